```python
import math
import jax, jax.numpy as jnp
from jax import lax
import numpy as np

D_MODEL = 1024
BATCH = 16
SEQ = 2048
DEPTH = 1

HEAD_DIM = 64
N_DIFF_HEADS = D_MODEL // HEAD_DIM // 2
DIFF_QK_DIM = HEAD_DIM // 2
N_DIL_HEADS = D_MODEL // HEAD_DIM // 2
N_HEADS_TOTAL = N_DIFF_HEADS + N_DIL_HEADS
MIX_WIDTH = N_HEADS_TOTAL * HEAD_DIM
DIL_PATTERNS = ((128, 1), (512, 4), (2048, 16))
Q_BLOCK = 128
N_EXPERTS = 32
TOP_K = 4
D_FF = D_MODEL
SWIGLU_LIMIT = 7.0
SWIGLU_ALPHA = 1.702
MOE_BLOCK = 256
NORM_EPS = 1e-5

DIFF_Q_COLS = N_DIFF_HEADS * 2 * DIFF_QK_DIM
DIFF_K_COLS = N_DIFF_HEADS * 2 * DIFF_QK_DIM
DIFF_V_COLS = N_DIFF_HEADS * HEAD_DIM
DIL_Q_COLS = N_DIL_HEADS * HEAD_DIM
DIL_K_COLS = N_DIL_HEADS * HEAD_DIM
DIL_V_COLS = N_DIL_HEADS * HEAD_DIM
IN_COLS = DIFF_Q_COLS + DIFF_K_COLS + DIFF_V_COLS + DIL_Q_COLS + DIL_K_COLS + DIL_V_COLS

kernel_name = "hybrid_diffattn_dilated_moe_block"


def rms_norm(x, w):
    xf = x.astype(jnp.float32)
    y = xf * lax.rsqrt(jnp.mean(xf * xf, axis=-1, keepdims=True) + NORM_EPS)
    return (y * w.astype(jnp.float32)).astype(x.dtype)


def alibi_slopes(n):
    return jnp.exp2(-8.0 * jnp.arange(1, n + 1, dtype=jnp.float32) / n)


def diff_attention(q, k, v, slopes, lam):
    S = q.shape[1]
    scale = DIFF_QK_DIM ** -0.5
    outs = []
    for i in range(S // Q_BLOCK):
        q0 = i * Q_BLOCK
        kend = q0 + Q_BLOCK
        qb, kb, vb = q[:, q0:kend], k[:, :kend], v[:, :kend]
        s = jnp.einsum('bqhmd,bkhmd->bhmqk', qb, kb,
                       preferred_element_type=jnp.float32) * scale
        dist = (q0 + jnp.arange(Q_BLOCK))[:, None] - jnp.arange(kend)[None, :]
        s = s - slopes[None, :, None, None, None] * dist.astype(jnp.float32)
        s = jnp.where(dist >= 0, s, -jnp.inf)
        p = jax.nn.softmax(s, axis=-1)
        pd = p[:, :, 0] - lam * p[:, :, 1]
        outs.append(jnp.einsum('bhqk,bkhd->bqhd', pd.astype(v.dtype), vb))
    return jnp.concatenate(outs, axis=1)


def dilated_pattern(q, k, v, slopes, window, dilation):
    B, S, H, Dh = q.shape
    L = S // dilation
    W = window // dilation
    nb = -(-L // W)
    Lp = nb * W
    N = B * dilation

    def to_sub(t):
        t = t.reshape(B, L, dilation, H, Dh).transpose(0, 2, 1, 3, 4).reshape(N, L, H, Dh)
        return jnp.pad(t, ((0, 0), (0, Lp - L), (0, 0), (0, 0)))

    qs, ks, vs = to_sub(q), to_sub(k), to_sub(v)
    qb = qs.reshape(N, nb, W, H, Dh)

    def kv_blocks(t):
        prev = jnp.pad(t, ((0, 0), (W, 0), (0, 0), (0, 0)))[:, :Lp].reshape(N, nb, W, H, Dh)
        cur = t.reshape(N, nb, W, H, Dh)
        return jnp.concatenate([prev, cur], axis=2)

    kb, vb = kv_blocks(ks), kv_blocks(vs)
    s = jnp.einsum('nbqhd,nbkhd->nbhqk', qb, kb,
                   preferred_element_type=jnp.float32) * (Dh ** -0.5)
    dist = jnp.arange(W)[:, None] - jnp.arange(2 * W)[None, :] + W
    kpos = (jnp.arange(nb) * W - W)[:, None] + jnp.arange(2 * W)[None, :]
    valid = ((dist >= 0) & (dist <= W))[None, :, :] & (kpos >= 0)[:, None, :]
    s = s - slopes[None, None, :, None, None] * (dist * dilation).astype(jnp.float32)
    s = jnp.where(valid[None, :, None, :, :], s, -jnp.inf)
    lse = jax.nn.logsumexp(s, axis=-1)
    p = jnp.exp(s - lse[..., None])
    o = jnp.einsum('nbhqk,nbkhd->nbqhd', p.astype(v.dtype), vb)
    o = o.reshape(N, Lp, H, Dh)[:, :L]
    o = o.reshape(B, dilation, L, H, Dh).transpose(0, 2, 1, 3, 4).reshape(B, S, H, Dh)
    lse = lse.transpose(0, 1, 3, 2).reshape(N, Lp, H)[:, :L]
    lse = lse.reshape(B, dilation, L, H).transpose(0, 2, 1, 3).reshape(B, S, H)
    return o, lse


def dilated_attention(q, k, v, slopes):
    outs, lses = [], []
    for window, dilation in DIL_PATTERNS:
        o, l = dilated_pattern(q, k, v, slopes, window, dilation)
        outs.append(o)
        lses.append(l)
    wts = jax.nn.softmax(jnp.stack(lses, axis=0), axis=0)
    o = jnp.sum(wts[..., None] * jnp.stack(outs, axis=0).astype(jnp.float32), axis=0)
    return o.astype(q.dtype)


def moe_ffn(h, router_w, router_b, w_gate_up, b_gate_up, w_down, b_down):
    Bb, S, D = h.shape
    T = Bb * S
    xf = h.reshape(T, D)
    logits = (xf @ router_w + router_b).astype(jnp.float32)
    top_v, top_i = lax.top_k(logits, TOP_K)
    gates = jax.nn.softmax(top_v, axis=-1)
    flat_e = top_i.reshape(-1)
    flat_t = jnp.arange(T * TOP_K, dtype=jnp.int32) // TOP_K
    flat_g = gates.reshape(-1)
    order = jnp.argsort(flat_e)
    se = flat_e[order]
    counts = jnp.bincount(flat_e, length=N_EXPERTS)
    starts = jnp.cumsum(counts) - counts
    pcounts = (counts + MOE_BLOCK - 1) // MOE_BLOCK * MOE_BLOCK
    pends = jnp.cumsum(pcounts)
    pstarts = pends - pcounts
    dest = pstarts[se] + jnp.arange(T * TOP_K) - starts[se]
    n_blocks = -(-(T * TOP_K + N_EXPERTS * (MOE_BLOCK - 1)) // MOE_BLOCK)
    R = n_blocks * MOE_BLOCK
    row_tok = jnp.full((R,), T, dtype=jnp.int32).at[dest].set(flat_t[order])
    row_gate = jnp.zeros((R,), dtype=h.dtype).at[dest].set(flat_g[order].astype(h.dtype))
    block_e = jnp.minimum(jnp.searchsorted(pends, jnp.arange(n_blocks) * MOE_BLOCK, side='right'),
                          N_EXPERTS - 1)
    x_pad = jnp.concatenate([xf, jnp.zeros((1, D), xf.dtype)], axis=0)

    def expert_block(args):
        tok, g, e = args
        xb = x_pad[tok]
        gu = xb @ w_gate_up[e] + b_gate_up[e]
        gate, up = gu[:, :D_FF], gu[:, D_FF:]
        gate = jnp.minimum(gate, SWIGLU_LIMIT)
        up = jnp.clip(up, -SWIGLU_LIMIT, SWIGLU_LIMIT)
        act = (up + 1.0) * (gate * jax.nn.sigmoid(SWIGLU_ALPHA * gate))
        return (act @ w_down[e] + b_down[e]) * g[:, None]

    out_rows = lax.map(expert_block, (row_tok.reshape(n_blocks, MOE_BLOCK),
                                      row_gate.reshape(n_blocks, MOE_BLOCK), block_e))
    y = jax.ops.segment_sum(out_rows.reshape(R, D), row_tok, num_segments=T + 1)[:T]
    return y.reshape(Bb, S, D)


def setup_inputs(seed: int = 0) -> dict:
    key = jax.random.key(seed)
    ks = jax.random.split(key, 18)
    f32 = jnp.float32
    nrm = lambda k, shape, scale: jax.random.normal(k, shape, f32) * scale
    return {
        "x": nrm(ks[0], (BATCH, SEQ, D_MODEL), 1.0),
        "attn_norm_w": 1.0 + nrm(ks[1], (DEPTH, D_MODEL), 0.02),
        "w_in": nrm(ks[2], (DEPTH, D_MODEL, IN_COLS), D_MODEL ** -0.5),
        "diff_lambda_q1": nrm(ks[3], (DEPTH, DIFF_QK_DIM), 0.1),
        "diff_lambda_k1": nrm(ks[4], (DEPTH, DIFF_QK_DIM), 0.1),
        "diff_lambda_q2": nrm(ks[5], (DEPTH, DIFF_QK_DIM), 0.1),
        "diff_lambda_k2": nrm(ks[6], (DEPTH, DIFF_QK_DIM), 0.1),
        "diff_subln_w": 1.0 + nrm(ks[7], (DEPTH, HEAD_DIM), 0.02),
        "w_out": nrm(ks[8], (DEPTH, MIX_WIDTH, D_MODEL), MIX_WIDTH ** -0.5),
        "ffn_norm_w": 1.0 + nrm(ks[9], (DEPTH, D_MODEL), 0.02),
        "router_w": nrm(ks[10], (DEPTH, D_MODEL, N_EXPERTS), D_MODEL ** -0.5),
        "router_b": nrm(ks[11], (DEPTH, N_EXPERTS), 0.01),
        "w_gate_up": nrm(ks[12], (DEPTH, N_EXPERTS, D_MODEL, 2 * D_FF), D_MODEL ** -0.5),
        "b_gate_up": nrm(ks[13], (DEPTH, N_EXPERTS, 2 * D_FF), 0.01),
        "w_down": nrm(ks[14], (DEPTH, N_EXPERTS, D_FF, D_MODEL), D_FF ** -0.5),
        "b_down": nrm(ks[15], (DEPTH, N_EXPERTS, D_MODEL), 0.01),
        "final_norm_w": 1.0 + nrm(ks[16], (D_MODEL,), 0.02),
    }


def reference(x, attn_norm_w, w_in, diff_lambda_q1, diff_lambda_k1, diff_lambda_q2,
              diff_lambda_k2, diff_subln_w, w_out, ffn_norm_w, router_w, router_b,
              w_gate_up, b_gate_up, w_down, b_down, final_norm_w):
    B, S, _ = x.shape
    slopes = alibi_slopes(N_HEADS_TOTAL)
    diff_slopes = slopes[0::2]
    dil_slopes = slopes[1::2]
    split_at = list(np.cumsum([DIFF_Q_COLS, DIFF_K_COLS, DIFF_V_COLS,
                               DIL_Q_COLS, DIL_K_COLS])[:])
    for l in range(DEPTH):
        h = rms_norm(x, attn_norm_w[l])
        proj = h @ w_in[l]
        dq, dk, dv, lq, lk, lv = jnp.split(proj, split_at, axis=-1)
        dq = dq.reshape(B, S, N_DIFF_HEADS, 2, DIFF_QK_DIM)
        dk = dk.reshape(B, S, N_DIFF_HEADS, 2, DIFF_QK_DIM)
        dv = dv.reshape(B, S, N_DIFF_HEADS, HEAD_DIM)
        lam_init = 0.8 - 0.6 * math.exp(-0.3 * l)
        lam = (jnp.exp(jnp.sum(diff_lambda_q1[l] * diff_lambda_k1[l]).astype(jnp.float32))
               - jnp.exp(jnp.sum(diff_lambda_q2[l] * diff_lambda_k2[l]).astype(jnp.float32))
               + lam_init)
        o_diff = diff_attention(dq, dk, dv, diff_slopes, lam)
        o_diff = rms_norm(o_diff, diff_subln_w[l]) * (1.0 - lam_init)
        lq = lq.reshape(B, S, N_DIL_HEADS, HEAD_DIM)
        lk = lk.reshape(B, S, N_DIL_HEADS, HEAD_DIM)
        lv = lv.reshape(B, S, N_DIL_HEADS, HEAD_DIM)
        o_dil = dilated_attention(lq, lk, lv, dil_slopes)
        mix = jnp.concatenate([o_diff.reshape(B, S, DIFF_V_COLS),
                               o_dil.reshape(B, S, DIL_V_COLS)], axis=-1)
        x = x + mix @ w_out[l]
        x = x + moe_ffn(rms_norm(x, ffn_norm_w[l]), router_w[l], router_b[l],
                        w_gate_up[l], b_gate_up[l], w_down[l], b_down[l])
    return rms_norm(x, final_norm_w)
```

```python
import functools
import math

import jax
import jax.numpy as jnp
from jax import lax
from jax.experimental import pallas as pl
from jax.experimental.pallas import tpu as pltpu

D_MODEL = 1024
HEAD_DIM = 64
N_DIFF_HEADS = 8
DIFF_QK_DIM = 32
N_DIL_HEADS = 8
N_HEADS_TOTAL = 16
DIL_PATTERNS = ((128, 1), (512, 4), (2048, 16))
DIL_W = 128
N_EXPERTS = 32
TOP_K = 4
D_FF = D_MODEL
SWIGLU_LIMIT = 7.0
SWIGLU_ALPHA = 1.702
NORM_EPS = 1e-5
IN_COLS = 3072
LANES = 128
MOE_BLK = 512
VMEM_LIMIT = 56 * 1024 * 1024

NEG_INF = float("-inf")


def _cparams(sem, vmem=VMEM_LIMIT):
    return pltpu.CompilerParams(dimension_semantics=sem, vmem_limit_bytes=vmem)


def _in_proj_kernel(x_ref, nw_ref, w_ref, cs_ref, o_ref, *, n_chunk):
    xf = x_ref[...]
    ms = jnp.mean(xf * xf, axis=-1, keepdims=True)
    h = (xf * lax.rsqrt(ms + NORM_EPS)) * nw_ref[...]
    hb = h.astype(jnp.bfloat16)
    n_cols = o_ref.shape[1]
    for c in range(n_cols // n_chunk):
        sl = slice(c * n_chunk, (c + 1) * n_chunk)
        r = jnp.dot(hb, w_ref[:, sl], preferred_element_type=jnp.float32)
        o_ref[:, sl] = (r * cs_ref[:, sl]).astype(o_ref.dtype)


def _in_proj(x2d, nw, w_bf16, colscale, tm=512):
    T, D = x2d.shape
    N = w_bf16.shape[1]
    return pl.pallas_call(
        functools.partial(_in_proj_kernel, n_chunk=512),
        grid=(T // tm,),
        in_specs=[
            pl.BlockSpec((tm, D), lambda i: (i, 0)),
            pl.BlockSpec((1, D), lambda i: (0, 0)),
            pl.BlockSpec((D, N), lambda i: (0, 0)),
            pl.BlockSpec((1, N), lambda i: (0, 0)),
        ],
        out_specs=pl.BlockSpec((tm, N), lambda i: (i, 0)),
        out_shape=jax.ShapeDtypeStruct((T, N), jnp.bfloat16),
        compiler_params=_cparams(("arbitrary",)),
        name="in_proj",
    )(x2d, nw, w_bf16, colscale)


def _diff_attn_kernel(slopes_ref, lam_ref, q_ref, k_ref, v_ref, w_ref, o_ref,
                      q4_ref, m_ref, l_ref, acc_ref, *, tq, tk, post_scale):
    p = pl.program_id(1)
    qi = pl.program_id(2)
    lane = lax.broadcasted_iota(jnp.int32, (tq, LANES), 1)
    qf = q_ref[...].astype(jnp.float32)
    for c in range(4):
        q4_ref[c * tq:(c + 1) * tq, :] = jnp.where(
            (lane >> 5) == c, qf, 0.0).astype(jnp.bfloat16)
    s0 = slopes_ref[2 * p]
    s1 = slopes_ref[2 * p + 1]
    rowi = lax.broadcasted_iota(jnp.int32, (4 * tq, 1), 0)
    slope4 = jnp.where(rowi < 2 * tq, s0, s1)
    rowpos = qi * tq + (rowi & (tq - 1))
    m_ref[...] = jnp.full(m_ref.shape, NEG_INF, jnp.float32)
    l_ref[...] = jnp.zeros(l_ref.shape, jnp.float32)
    acc_ref[...] = jnp.zeros(acc_ref.shape, jnp.float32)

    def step(j, masked):
        start = pl.multiple_of(j * tk, tk)
        kj = k_ref[pl.ds(start, tk), :]
        vj = v_ref[pl.ds(start, tk), :]
        s = lax.dot_general(q4_ref[...], kj, (((1,), (1,)), ((), ())),
                            preferred_element_type=jnp.float32)
        col = j * tk + lax.broadcasted_iota(jnp.int32, (1, tk), 1)
        s = s + slope4 * col.astype(jnp.float32)
        if masked:
            s = jnp.where(col <= rowpos, s, NEG_INF)
        m_prev = m_ref[...]
        m_new = jnp.maximum(m_prev, jnp.max(s, axis=1, keepdims=True))
        alpha = jnp.exp(m_prev - m_new)
        pr = jnp.exp(s - m_new)
        l_ref[...] = alpha * l_ref[...] + jnp.sum(pr, axis=1, keepdims=True)
        acc_ref[...] = alpha * acc_ref[...] + jnp.dot(
            pr.astype(jnp.bfloat16), vj, preferred_element_type=jnp.float32)
        m_ref[...] = m_new

    def body(j, carry):
        step(j, False)
        return carry

    lax.fori_loop(0, qi, body, 0)
    step(qi, True)

    lam = lam_ref[0]
    on = acc_ref[...] / l_ref[...]
    o0 = on[0:tq] - lam * on[tq:2 * tq]
    o1 = on[2 * tq:3 * tq] - lam * on[3 * tq:4 * tq]
    first = lane < HEAD_DIM
    o = jnp.where(first, o0, o1)
    o2 = o * o
    ms0 = jnp.sum(jnp.where(first, o2, 0.0), axis=1, keepdims=True)
    ms1 = jnp.sum(jnp.where(first, 0.0, o2), axis=1, keepdims=True)
    ms = jnp.where(first, ms0, ms1) * (1.0 / HEAD_DIM)
    y = (o * lax.rsqrt(ms + NORM_EPS)) * w_ref[...]
    o_ref[...] = (y * post_scale).astype(o_ref.dtype)


def _diff_attn(proj, slopes, lam, subln_w2, B, S, post_scale, tq=256, tk=256):
    T = proj.shape[0]
    nq = S // tq
    n_pairs = N_DIFF_HEADS // 2
    kern = functools.partial(_diff_attn_kernel, tq=tq, tk=tk, post_scale=post_scale)
    return pl.pallas_call(
        kern,
        grid=(B, n_pairs, nq),
        in_specs=[
            pl.BlockSpec(memory_space=pltpu.SMEM),
            pl.BlockSpec(memory_space=pltpu.SMEM),
            pl.BlockSpec((tq, LANES), lambda b, p, i: (b * nq + i, p)),
            pl.BlockSpec((S, LANES), lambda b, p, i: (b, n_pairs + p)),
            pl.BlockSpec((S, LANES), lambda b, p, i: (b, 2 * n_pairs + p)),
            pl.BlockSpec((1, LANES), lambda b, p, i: (0, 0)),
        ],
        out_specs=pl.BlockSpec((tq, LANES), lambda b, p, i: (b * nq + i, p)),
        out_shape=jax.ShapeDtypeStruct((T, N_DIFF_HEADS * HEAD_DIM), jnp.bfloat16),
        scratch_shapes=[
            pltpu.VMEM((4 * tq, LANES), jnp.bfloat16),
            pltpu.VMEM((4 * tq, 1), jnp.float32),
            pltpu.VMEM((4 * tq, 1), jnp.float32),
            pltpu.VMEM((4 * tq, LANES), jnp.float32),
        ],
        compiler_params=_cparams(("arbitrary", "arbitrary", "arbitrary")),
        name="diff_attn",
    )(slopes, lam, proj, proj, proj, subln_w2)


def _dil_attn_kernel(slopes_ref, q_ref, k_ref, v_ref, o_ref,
                     qf_ref, kf_ref, vf_ref, qd_ref, kd_ref, vd_ref, a_ref,
                     po_ref, pl_ref, ro_ref, rl_ref, *, S):
    p = pl.program_id(1)
    W = DIL_W
    n_blocks = S // W
    qf_ref[...] = q_ref[...].astype(jnp.float32)
    kf_ref[...] = k_ref[...].astype(jnp.float32)
    vf_ref[...] = v_ref[...].astype(jnp.float32)
    kd_ref[0:W, :] = jnp.zeros((W, LANES), jnp.bfloat16)
    vd_ref[0:W, :] = jnp.zeros((W, LANES), jnp.bfloat16)
    lane = lax.broadcasted_iota(jnp.int32, (W, LANES), 1)
    first = lane < HEAD_DIM
    r_i = lax.broadcasted_iota(jnp.int32, (W, 2 * W), 0)
    c_i = lax.broadcasted_iota(jnp.int32, (W, 2 * W), 1)
    dist = W + r_i - c_i
    valid = (dist >= 0) & (dist <= W)
    dist_f = dist.astype(jnp.float32)

    for pi, (_, d) in enumerate(DIL_PATTERNS):
        L = S // d
        nbk = L // W
        if d == 1:
            qd_ref[...] = q_ref[...]
            kd_ref[W:W + S, :] = k_ref[...]
            vd_ref[W:W + S, :] = v_ref[...]
        else:
            for r in range(d):
                qd_ref[r * L:(r + 1) * L, :] = qf_ref[pl.ds(r, L, stride=d), :].astype(jnp.bfloat16)
                kd_ref[W + r * L:W + (r + 1) * L, :] = kf_ref[pl.ds(r, L, stride=d), :].astype(jnp.bfloat16)
                vd_ref[W + r * L:W + (r + 1) * L, :] = vf_ref[pl.ds(r, L, stride=d), :].astype(jnp.bfloat16)
        for h in range(2):
            slope = slopes_ref[2 * p + h] * float(d)
            full = jnp.where(valid, slope * dist_f, jnp.inf)
            a_ref[h] = full
            a_ref[2 + h] = jnp.where(c_i < W, jnp.inf, full)

        def block(u, carry, nbk=nbk):
            row0 = pl.multiple_of(u * W, W)
            no_prev = 1 if nbk == 1 else (lax.rem(u, nbk) == 0).astype(jnp.int32)
            qb = qd_ref[pl.ds(row0, W), :].astype(jnp.float32)
            kk = kd_ref[pl.ds(row0, 2 * W), :]
            vv = vd_ref[pl.ds(row0, 2 * W), :]
            outs = []
            lses = []
            for h in range(2):
                qh = jnp.where(first if h == 0 else jnp.logical_not(first), qb, 0.0)
                s = lax.dot_general(qh.astype(jnp.bfloat16), kk, (((1,), (1,)), ((), ())),
                                    preferred_element_type=jnp.float32)
                s = s - a_ref[2 * no_prev + h]
                m = jnp.max(s, axis=1, keepdims=True)
                pr = jnp.exp(s - m)
                l = jnp.sum(pr, axis=1, keepdims=True)
                pv = jnp.dot(pr.astype(jnp.bfloat16), vv, preferred_element_type=jnp.float32)
                outs.append(pv / l)
                lses.append(m + jnp.log(l))
            po_ref[pl.ds(row0, W), :] = jnp.where(first, outs[0], outs[1])
            pl_ref[pl.ds(row0, W), :] = jnp.where(first, lses[0], lses[1])
            return carry

        lax.fori_loop(0, n_blocks, block, 0)

        if d == 1:
            ro_ref[pi] = po_ref[...]
            rl_ref[pi] = pl_ref[...]
        else:
            for r in range(d):
                ro_ref[pi, pl.ds(r, L, stride=d), :] = po_ref[r * L:(r + 1) * L, :]
                rl_ref[pi, pl.ds(r, L, stride=d), :] = pl_ref[r * L:(r + 1) * L, :]

    def mix(c, carry):
        row0 = pl.multiple_of(c * W, W)
        ls = [rl_ref[i, pl.ds(row0, W), :] for i in range(3)]
        mx = jnp.maximum(jnp.maximum(ls[0], ls[1]), ls[2])
        es = [jnp.exp(x - mx) for x in ls]
        den = es[0] + es[1] + es[2]
        num = (es[0] * ro_ref[0, pl.ds(row0, W), :] + es[1] * ro_ref[1, pl.ds(row0, W), :]
               + es[2] * ro_ref[2, pl.ds(row0, W), :])
        o_ref[pl.ds(row0, W), :] = (num / den).astype(o_ref.dtype)
        return carry

    lax.fori_loop(0, n_blocks, mix, 0)


def _dil_attn(proj, slopes, B, S):
    T = proj.shape[0]
    n_pairs = N_DIL_HEADS // 2
    base = 3 * N_DIFF_HEADS * HEAD_DIM // LANES
    W = DIL_W
    return pl.pallas_call(
        functools.partial(_dil_attn_kernel, S=S),
        grid=(B, n_pairs),
        in_specs=[
            pl.BlockSpec(memory_space=pltpu.SMEM),
            pl.BlockSpec((S, LANES), lambda b, p: (b, base + p)),
            pl.BlockSpec((S, LANES), lambda b, p: (b, base + n_pairs + p)),
            pl.BlockSpec((S, LANES), lambda b, p: (b, base + 2 * n_pairs + p)),
        ],
        out_specs=pl.BlockSpec((S, LANES), lambda b, p: (b, p)),
        out_shape=jax.ShapeDtypeStruct((T, N_DIL_HEADS * HEAD_DIM), jnp.bfloat16),
        scratch_shapes=[
            pltpu.VMEM((S, LANES), jnp.float32),
            pltpu.VMEM((S, LANES), jnp.float32),
            pltpu.VMEM((S, LANES), jnp.float32),
            pltpu.VMEM((S, LANES), jnp.bfloat16),
            pltpu.VMEM((S + W, LANES), jnp.bfloat16),
            pltpu.VMEM((S + W, LANES), jnp.bfloat16),
            pltpu.VMEM((4, W, 2 * W), jnp.float32),
            pltpu.VMEM((S, LANES), jnp.float32),
            pltpu.VMEM((S, LANES), jnp.float32),
            pltpu.VMEM((3, S, LANES), jnp.float32),
            pltpu.VMEM((3, S, LANES), jnp.float32),
        ],
        compiler_params=_cparams(("arbitrary", "arbitrary")),
        name="dil_attn",
    )(slopes, proj, proj, proj)


def _post_attn_kernel(x_ref, md_ref, ml_ref, wo_ref, nw_ref, rwh_ref, rwl_ref, rb_ref,
                      x1_ref, h2_ref, route_ref, gate_ref, cnt_ref, run_ref, *, tm):
    i = pl.program_id(0)

    @pl.when(i == 0)
    def _():
        run_ref[...] = jnp.zeros(run_ref.shape, jnp.float32)

    half = md_ref.shape[1]
    y = jnp.dot(md_ref[...], wo_ref[0:half, :], preferred_element_type=jnp.float32)
    y = y + jnp.dot(ml_ref[...], wo_ref[half:, :], preferred_element_type=jnp.float32)
    x1 = x_ref[...] + y
    x1_ref[...] = x1
    ms = jnp.mean(x1 * x1, axis=-1, keepdims=True)
    h2 = (x1 * lax.rsqrt(ms + NORM_EPS)) * nw_ref[...]
    h2_ref[...] = h2
    hh = h2.astype(jnp.bfloat16)
    hl = (h2 - hh.astype(jnp.float32)).astype(jnp.bfloat16)
    logits = jnp.dot(hh, rwh_ref[...], preferred_element_type=jnp.float32)
    logits = logits + jnp.dot(hl, rwh_ref[...], preferred_element_type=jnp.float32)
    logits = logits + jnp.dot(hh, rwl_ref[...], preferred_element_type=jnp.float32)
    logits = logits + rb_ref[...]
    lane = lax.broadcasted_iota(jnp.int32, (tm, LANES), 1)
    work = jnp.where(lane < N_EXPERTS, logits, NEG_INF)
    vals, idxs, sels = [], [], []
    for _ in range(TOP_K):
        mk = jnp.max(work, axis=1, keepdims=True)
        ik = jnp.min(jnp.where(work == mk, lane, LANES), axis=1, keepdims=True)
        sel = lane == ik
        work = jnp.where(sel, NEG_INF, work)
        vals.append(mk)
        idxs.append(ik)
        sels.append(sel)
    es = [jnp.exp(v - vals[0]) for v in vals]
    den = es[0] + es[1] + es[2] + es[3]
    member = jnp.zeros((tm, LANES), jnp.float32)
    for sel in sels:
        member = jnp.where(sel, 1.0, member)
    r_i = lax.broadcasted_iota(jnp.int32, (tm, tm), 0)
    c_i = lax.broadcasted_iota(jnp.int32, (tm, tm), 1)
    tri = jnp.where(c_i < r_i, 1.0, 0.0).astype(jnp.bfloat16)
    before = jnp.dot(tri, member.astype(jnp.bfloat16), preferred_element_type=jnp.float32)
    before = before + run_ref[...]
    route = jnp.zeros((tm, LANES), jnp.int32)
    gates = jnp.zeros((tm, LANES), jnp.float32)
    for k in range(TOP_K):
        rank = jnp.sum(jnp.where(sels[k], before, 0.0), axis=1, keepdims=True)
        route = jnp.where(lane == k, idxs[k], route)
        route = jnp.where(lane == TOP_K + k, rank.astype(jnp.int32), route)
        gates = jnp.where(lane == k, es[k] / den, gates)
    route_ref[...] = route
    gate_ref[...] = gates
    run_new = run_ref[...] + jnp.sum(member, axis=0, keepdims=True)
    run_ref[...] = run_new
    cnt_ref[...] = jnp.broadcast_to(run_new, cnt_ref.shape).astype(jnp.int32)


def _post_attn(x2d, mix_d, mix_l, wo_bf16, nw, rw_hi, rw_lo, rb, tm=512):
    T, D = x2d.shape
    half = mix_d.shape[1]
    return pl.pallas_call(
        functools.partial(_post_attn_kernel, tm=tm),
        grid=(T // tm,),
        in_specs=[
            pl.BlockSpec((tm, D), lambda i: (i, 0)),
            pl.BlockSpec((tm, half), lambda i: (i, 0)),
            pl.BlockSpec((tm, half), lambda i: (i, 0)),
            pl.BlockSpec((D, D), lambda i: (0, 0)),
            pl.BlockSpec((1, D), lambda i: (0, 0)),
            pl.BlockSpec((D, LANES), lambda i: (0, 0)),
            pl.BlockSpec((D, LANES), lambda i: (0, 0)),
            pl.BlockSpec((1, LANES), lambda i: (0, 0)),
        ],
        out_specs=[
            pl.BlockSpec((tm, D), lambda i: (i, 0)),
            pl.BlockSpec((tm, D), lambda i: (i, 0)),
            pl.BlockSpec((tm, LANES), lambda i: (i, 0)),
            pl.BlockSpec((tm, LANES), lambda i: (i, 0)),
            pl.BlockSpec((8, LANES), lambda i: (0, 0)),
        ],
        out_shape=[
            jax.ShapeDtypeStruct((T, D), jnp.float32),
            jax.ShapeDtypeStruct((T, D), jnp.float32),
            jax.ShapeDtypeStruct((T, LANES), jnp.int32),
            jax.ShapeDtypeStruct((T, LANES), jnp.float32),
            jax.ShapeDtypeStruct((8, LANES), jnp.int32),
        ],
        scratch_shapes=[pltpu.VMEM((1, LANES), jnp.float32)],
        compiler_params=_cparams(("arbitrary",)),
        name="post_attn",
    )(x2d, mix_d, mix_l, wo_bf16, nw, rw_hi, rw_lo, rb)


def _dispatch_kernel(dest_ref, cnt_ref, pcnt_ref, pst_ref, h_ref, xs_ref, z_ref, sem, zsem, *, td):
    i = pl.program_id(0)
    n = pl.num_programs(0)

    def row_copy(t, d):
        return pltpu.make_async_copy(h_ref.at[pl.ds(t, 1)], xs_ref.at[pl.ds(d, 1)], sem)

    def issue(t, carry):
        for k in range(TOP_K):
            row_copy(t, dest_ref[(i * td + t) * TOP_K + k]).start()
        return carry

    lax.fori_loop(0, td, issue, 0)

    def drain(t, carry):
        for k in range(TOP_K):
            row_copy(t, 0).wait()
        return carry

    lax.fori_loop(0, td, drain, 0)

    @pl.when(i == n - 1)
    def _():
        z_ref[...] = jnp.zeros(z_ref.shape, z_ref.dtype)

        def pad_copy(row):
            return pltpu.make_async_copy(z_ref.at[pl.ds(0, 1)], xs_ref.at[pl.ds(row, 1)], zsem)

        def per_expert(e, carry):
            base = pst_ref[e]

            def start(r, c):
                pad_copy(base + r).start()
                return c

            def wait(r, c):
                pad_copy(base + r).wait()
                return c

            lax.fori_loop(cnt_ref[e], pcnt_ref[e], start, 0)
            lax.fori_loop(cnt_ref[e], pcnt_ref[e], wait, 0)
            return carry

        lax.fori_loop(0, N_EXPERTS, per_expert, 0)

        def tail_copy(c):
            row = pl.multiple_of(c * td, td)
            return pltpu.make_async_copy(z_ref, xs_ref.at[pl.ds(row, td)], zsem)

        end = pst_ref[N_EXPERTS - 1] + pcnt_ref[N_EXPERTS - 1]
        first_chunk = end // td
        n_chunks = xs_ref.shape[0] // td

        def tail_start(c, carry):
            tail_copy(c).start()
            return carry

        def tail_wait(c, carry):
            tail_copy(c).wait()
            return carry

        lax.fori_loop(first_chunk, n_chunks, tail_start, 0)
        lax.fori_loop(first_chunk, n_chunks, tail_wait, 0)


def _dispatch(dest_flat, counts, pcounts, pstarts, h2, R, td=256):
    T, D = h2.shape
    return pl.pallas_call(
        functools.partial(_dispatch_kernel, td=td),
        grid_spec=pltpu.PrefetchScalarGridSpec(
            num_scalar_prefetch=4,
            grid=(T // td,),
            in_specs=[pl.BlockSpec((td, D), lambda i, *_: (i, 0))],
            out_specs=pl.BlockSpec(memory_space=pl.ANY),
            scratch_shapes=[
                pltpu.VMEM((td, D), h2.dtype),
                pltpu.SemaphoreType.DMA,
                pltpu.SemaphoreType.DMA,
            ],
        ),
        out_shape=jax.ShapeDtypeStruct((R, D), h2.dtype),
        compiler_params=_cparams(("arbitrary",)),
        name="dispatch",
    )(dest_flat, counts, pcounts, pstarts, h2)


def _experts_kernel(be_ref, used_ref, xs_ref, wgu_ref, bgu_ref, wd_ref, bd_ref, o_ref,
                    wgu_bf, wd_bf, *, n_chunk):
    i = pl.program_id(0)
    prev = be_ref[jnp.maximum(i - 1, 0)]
    new_expert = jnp.logical_or(i == 0, be_ref[i] != prev)

    @pl.when(new_expert)
    def _():
        wgu_bf[...] = wgu_ref[0].astype(jnp.bfloat16)
        wd_bf[...] = wd_ref[0].astype(jnp.bfloat16)

    @pl.when(i < used_ref[0])
    def _():
        xb = xs_ref[...].astype(jnp.bfloat16)
        acc = jnp.zeros(o_ref.shape, jnp.float32)
        for c in range(D_FF // n_chunk):
            gs = slice(c * n_chunk, (c + 1) * n_chunk)
            us = slice(D_FF + c * n_chunk, D_FF + (c + 1) * n_chunk)
            g = jnp.dot(xb, wgu_bf[:, gs], preferred_element_type=jnp.float32) + bgu_ref[0, :, gs]
            u = jnp.dot(xb, wgu_bf[:, us], preferred_element_type=jnp.float32) + bgu_ref[0, :, us]
            g = jnp.minimum(g, SWIGLU_LIMIT)
            u = jnp.clip(u, -SWIGLU_LIMIT, SWIGLU_LIMIT)
            act = (u + 1.0) * (g * jax.nn.sigmoid(SWIGLU_ALPHA * g))
            acc = acc + jnp.dot(act.astype(jnp.bfloat16), wd_bf[gs, :],
                                preferred_element_type=jnp.float32)
        o_ref[...] = acc + bd_ref[0]

    @pl.when(i >= used_ref[0])
    def _():
        o_ref[...] = jnp.zeros(o_ref.shape, o_ref.dtype)


def _experts(block_e, used, xs, w_gate_up, b_gate_up, w_down, b_down, blk=MOE_BLK):
    R, D = xs.shape
    n_blocks = R // blk
    F2 = w_gate_up.shape[2]

    def row_map(i, be, used):
        return (i, 0)

    return pl.pallas_call(
        functools.partial(_experts_kernel, n_chunk=512),
        grid_spec=pltpu.PrefetchScalarGridSpec(
            num_scalar_prefetch=2,
            grid=(n_blocks,),
            in_specs=[
                pl.BlockSpec((blk, D), row_map),
                pl.BlockSpec((1, D, F2), lambda i, be, used: (be[i], 0, 0)),
                pl.BlockSpec((1, 1, F2), lambda i, be, used: (be[i], 0, 0)),
                pl.BlockSpec((1, D_FF, D), lambda i, be, used: (be[i], 0, 0)),
                pl.BlockSpec((1, 1, D), lambda i, be, used: (be[i], 0, 0)),
            ],
            out_specs=pl.BlockSpec((blk, D), row_map),
            scratch_shapes=[
                pltpu.VMEM((D, F2), jnp.bfloat16),
                pltpu.VMEM((D_FF, D), jnp.bfloat16),
            ],
        ),
        out_shape=jax.ShapeDtypeStruct((R, D), jnp.float32),
        compiler_params=_cparams(("arbitrary",)),
        name="experts",
    )(block_e, used, xs, w_gate_up, b_gate_up.reshape(N_EXPERTS, 1, F2),
      w_down, b_down.reshape(N_EXPERTS, 1, D))


def _combine_kernel(dest_ref, x1_ref, g_ref, nw_ref, rows_ref, o_ref, buf_ref, sem, *, tc):
    i = pl.program_id(0)

    def row_copy(t, k, d):
        return pltpu.make_async_copy(rows_ref.at[pl.ds(d, 1)], buf_ref.at[k, pl.ds(t, 1)], sem)

    def issue(t, carry):
        for k in range(TOP_K):
            row_copy(t, k, dest_ref[(i * tc + t) * TOP_K + k]).start()
        return carry

    lax.fori_loop(0, tc, issue, 0)

    def drain(t, carry):
        for k in range(TOP_K):
            row_copy(t, k, 0).wait()
        return carry

    lax.fori_loop(0, tc, drain, 0)

    g = g_ref[...]
    y = x1_ref[...]
    for k in range(TOP_K):
        y = y + g[:, k:k + 1] * buf_ref[k]
    ms = jnp.mean(y * y, axis=-1, keepdims=True)
    o_ref[...] = (y * lax.rsqrt(ms + NORM_EPS)) * nw_ref[...]


def _combine(dest_flat, x1, gates, nw, rows, tc=128):
    T, D = x1.shape
    return pl.pallas_call(
        functools.partial(_combine_kernel, tc=tc),
        grid_spec=pltpu.PrefetchScalarGridSpec(
            num_scalar_prefetch=1,
            grid=(T // tc,),
            in_specs=[
                pl.BlockSpec((tc, D), lambda i, *_: (i, 0)),
                pl.BlockSpec((tc, LANES), lambda i, *_: (i, 0)),
                pl.BlockSpec((1, D), lambda i, *_: (0, 0)),
                pl.BlockSpec(memory_space=pl.ANY),
            ],
            out_specs=pl.BlockSpec((tc, D), lambda i, *_: (i, 0)),
            scratch_shapes=[
                pltpu.VMEM((TOP_K, tc, D), jnp.float32),
                pltpu.SemaphoreType.DMA,
            ],
        ),
        out_shape=jax.ShapeDtypeStruct((T, D), jnp.float32),
        compiler_params=_cparams(("arbitrary",)),
        name="combine",
    )(dest_flat, x1, gates, nw, rows)


def kernel(x, attn_norm_w, w_in, diff_lambda_q1, diff_lambda_k1, diff_lambda_q2, diff_lambda_k2,
           diff_subln_w, w_out, ffn_norm_w, router_w, router_b, w_gate_up, b_gate_up, w_down,
           b_down, final_norm_w):
    B, S, D = x.shape
    T = B * S
    f32 = jnp.float32
    n = jnp.arange(1, N_HEADS_TOTAL + 1, dtype=f32)
    slopes = jnp.exp2(-8.0 * n / N_HEADS_TOTAL)
    diff_slopes = slopes[0::2]
    dil_slopes = slopes[1::2]
    depth = attn_norm_w.shape[0]
    assert depth == 1, "the combine kernel fuses the final norm, so exactly one layer is supported"
    x2d = x.reshape(T, D)
    diff_w = N_DIFF_HEADS * HEAD_DIM
    colscale = jnp.concatenate([
        jnp.full((diff_w,), DIFF_QK_DIM ** -0.5, f32), jnp.ones((2 * diff_w,), f32),
        jnp.full((N_DIL_HEADS * HEAD_DIM,), HEAD_DIM ** -0.5, f32),
        jnp.ones((2 * N_DIL_HEADS * HEAD_DIM,), f32)]).reshape(1, IN_COLS)
    n_blocks = -(-(T * TOP_K + N_EXPERTS * (MOE_BLK - 1)) // MOE_BLK)
    R = n_blocks * MOE_BLK

    for l in range(depth):
        proj = _in_proj(x2d, attn_norm_w[l].reshape(1, D), w_in[l].astype(jnp.bfloat16), colscale)
        lam_init = 0.8 - 0.6 * math.exp(-0.3 * l)
        lam = (jnp.exp(jnp.sum(diff_lambda_q1[l] * diff_lambda_k1[l]).astype(f32))
               - jnp.exp(jnp.sum(diff_lambda_q2[l] * diff_lambda_k2[l]).astype(f32))
               + lam_init).reshape(1)
        subln2 = jnp.tile(diff_subln_w[l], 2).reshape(1, LANES)
        mix_d = _diff_attn(proj, diff_slopes, lam, subln2, B, S, 1.0 - lam_init)
        mix_l = _dil_attn(proj, dil_slopes, B, S)

        rw = jnp.zeros((D, LANES), f32).at[:, :N_EXPERTS].set(router_w[l])
        rw_hi = rw.astype(jnp.bfloat16)
        rw_lo = (rw - rw_hi.astype(f32)).astype(jnp.bfloat16)
        rb = jnp.zeros((1, LANES), f32).at[0, :N_EXPERTS].set(router_b[l])
        x1, h2, route, gates, cnt = _post_attn(
            x2d, mix_d, mix_l, w_out[l].astype(jnp.bfloat16), ffn_norm_w[l].reshape(1, D),
            rw_hi, rw_lo, rb)

        counts = cnt[0, :N_EXPERTS]
        pcounts = (counts + MOE_BLK - 1) // MOE_BLK * MOE_BLK
        pends = jnp.cumsum(pcounts)
        pstarts = pends - pcounts
        top_i = route[:, :TOP_K]
        rank = route[:, TOP_K:2 * TOP_K]
        onehot = top_i[..., None] == jnp.arange(N_EXPERTS, dtype=jnp.int32)
        dest = jnp.sum(jnp.where(onehot, pstarts.astype(jnp.int32), 0), axis=-1) + rank
        dest_flat = dest.reshape(T * TOP_K).astype(jnp.int32)
        used = (pends[-1] // MOE_BLK).astype(jnp.int32)
        blk_start = jnp.arange(n_blocks, dtype=jnp.int32) * MOE_BLK
        be = jnp.minimum(jnp.sum(pends[None, :] <= blk_start[:, None], axis=1), N_EXPERTS - 1)
        be_last = jnp.max(jnp.where(jnp.arange(n_blocks) < used, be, 0))
        block_e = jnp.where(jnp.arange(n_blocks) < used, be, be_last).astype(jnp.int32)

        xs = _dispatch(dest_flat, counts.astype(jnp.int32), pcounts.astype(jnp.int32),
                       pstarts.astype(jnp.int32), h2, R)
        rows = _experts(block_e, used.reshape(1), xs, w_gate_up[l], b_gate_up[l], w_down[l], b_down[l])
        x2d = _combine(dest_flat, x1, gates, final_norm_w.reshape(1, D), rows)
    return x2d.reshape(B, S, D)
```

```python
import functools
import math

import jax
import jax.numpy as jnp
from jax import lax
from jax.experimental import pallas as pl
from jax.experimental.pallas import tpu as pltpu

D_MODEL = 1024
HEAD_DIM = 64
N_DIFF_HEADS = 8
DIFF_QK_DIM = 32
N_DIL_HEADS = 8
N_HEADS_TOTAL = 16
DIL_PATTERNS = ((128, 1), (512, 4), (2048, 16))
DIL_W = 128
N_EXPERTS = 32
TOP_K = 4
D_FF = D_MODEL
SWIGLU_LIMIT = 7.0
SWIGLU_ALPHA = 1.702
NORM_EPS = 1e-5
IN_COLS = 3072
LANES = 128
DIL_UNROLL = 4
LOG2E = math.log2(math.e)
MOE_BLK = 512
VMEM_LIMIT = 56 * 1024 * 1024

NEG_INF = float("-inf")


def _cparams(sem, vmem=VMEM_LIMIT):
    return pltpu.CompilerParams(dimension_semantics=sem, vmem_limit_bytes=vmem)


def _in_proj_kernel(x_ref, nw_ref, w_ref, cs_ref, o_ref, *, n_chunk):
    xf = x_ref[...]
    ms = jnp.mean(xf * xf, axis=-1, keepdims=True)
    h = (xf * lax.rsqrt(ms + NORM_EPS)) * nw_ref[...]
    hb = h.astype(jnp.bfloat16)
    n_cols = o_ref.shape[1]
    for c in range(n_cols // n_chunk):
        sl = slice(c * n_chunk, (c + 1) * n_chunk)
        r = jnp.dot(hb, w_ref[:, sl], preferred_element_type=jnp.float32)
        o_ref[:, sl] = (r * cs_ref[:, sl]).astype(o_ref.dtype)


def _in_proj(x2d, nw, w_bf16, colscale, tm=512):
    T, D = x2d.shape
    N = w_bf16.shape[1]
    return pl.pallas_call(
        functools.partial(_in_proj_kernel, n_chunk=512),
        grid=(T // tm,),
        in_specs=[
            pl.BlockSpec((tm, D), lambda i: (i, 0)),
            pl.BlockSpec((1, D), lambda i: (0, 0)),
            pl.BlockSpec((D, N), lambda i: (0, 0)),
            pl.BlockSpec((1, N), lambda i: (0, 0)),
        ],
        out_specs=pl.BlockSpec((tm, N), lambda i: (i, 0)),
        out_shape=jax.ShapeDtypeStruct((T, N), jnp.bfloat16),
        compiler_params=_cparams(("arbitrary",)),
        name="in_proj",
    )(x2d, nw, w_bf16, colscale)


def _diff_attn_kernel(slopes_ref, lam_ref, q_ref, k_ref, v_ref, w_ref, o_ref,
                      q4t_ref, vt_ref, bias_ref, m_ref, l_ref, acc_ref, *, tq, tk, post_scale):
    assert tq == tk
    p = pl.program_id(1)
    qi = pl.program_id(2)
    n_lanes = 4 * tq
    n_kv = v_ref.shape[0] // tk

    lane = lax.broadcasted_iota(jnp.int32, (1, n_lanes), 1)
    slope_l = jnp.where(lane < 2 * tq, slopes_ref[2 * p], slopes_ref[2 * p + 1])

    @pl.when(qi == 0)
    def _():
        for jj in range(n_kv):
            vt = v_ref[jj * tk:(jj + 1) * tk, :].astype(jnp.float32).T
            vt_ref[jj] = vt.astype(jnp.bfloat16)
        key_i = lax.broadcasted_iota(jnp.int32, (tk, n_lanes), 0)
        qry_i = lax.broadcasted_iota(jnp.int32, (tk, n_lanes), 1) & (tq - 1)
        bias = key_i.astype(jnp.float32) * slope_l
        bias_ref[0] = bias
        bias_ref[1] = jnp.where(key_i <= qry_i, bias, NEG_INF)

    qt = q_ref[...].astype(jnp.float32).T
    sub = lax.broadcasted_iota(jnp.int32, (LANES, tq), 0) >> 5
    for c in range(4):
        q4t_ref[:, c * tq:(c + 1) * tq] = jnp.where(sub == c, qt, 0.0).astype(jnp.bfloat16)
    acc_ref[...] = jnp.zeros(acc_ref.shape, jnp.float32)

    def step(j, diag, m_prev, l_prev):
        start = pl.multiple_of(j * tk, tk)
        kj = k_ref[pl.ds(start, tk), :]
        s = jnp.dot(kj, q4t_ref[...], preferred_element_type=jnp.float32)
        s = s + bias_ref[1 if diag else 0]
        off = slope_l * (j * tk).astype(jnp.float32)
        m_new = jnp.maximum(m_prev, jnp.max(s, axis=0, keepdims=True) + off)
        alpha = jnp.exp2(m_prev - m_new)
        pr = jnp.exp2(s - (m_new - off))
        l_new = alpha * l_prev + jnp.sum(pr, axis=0, keepdims=True)
        pb = pr.astype(jnp.bfloat16)
        vt = vt_ref[j]
        for h in range(2):
            cols = slice(h * 2 * tq, (h + 1) * 2 * tq)
            pv = jnp.dot(vt[h * HEAD_DIM:(h + 1) * HEAD_DIM, :], pb[:, cols],
                         preferred_element_type=jnp.float32)
            acc_ref[h] = alpha[:, cols] * acc_ref[h] + pv
        return m_new, l_new

    def pair(u, carry):
        m, l = step(2 * u, False, *carry)
        return step(2 * u + 1, False, m, l)

    init = (jnp.full((1, n_lanes), NEG_INF, jnp.float32), jnp.zeros((1, n_lanes), jnp.float32))
    m, l = lax.fori_loop(0, qi // 2, pair, init)
    m_ref[...] = m
    l_ref[...] = l

    @pl.when((qi & 1) == 1)
    def _():
        m1, l1 = step(qi - 1, False, m_ref[...], l_ref[...])
        m2, l2 = step(qi, True, m1, l1)
        m_ref[...] = m2
        l_ref[...] = l2

    @pl.when((qi & 1) == 0)
    def _():
        m2, l2 = step(qi, True, m_ref[...], l_ref[...])
        m_ref[...] = m2
        l_ref[...] = l2

    lam = lam_ref[0]
    l = l_ref[...]
    ys = []
    for h in range(2):
        acc = acc_ref[h]
        c0 = h * 2 * tq
        o = acc[:, :tq] / l[:, c0:c0 + tq] - lam * (acc[:, tq:] / l[:, c0 + tq:c0 + 2 * tq])
        ms = jnp.mean(o * o, axis=0, keepdims=True)
        ys.append(o * lax.rsqrt(ms + NORM_EPS))
    y = jnp.concatenate(ys, axis=0).T
    o_ref[...] = ((y * w_ref[...]) * post_scale).astype(o_ref.dtype)


def _diff_attn(proj, slopes, lam, subln_w2, B, S, post_scale, tq=256, tk=256):
    T = proj.shape[0]
    nq = S // tq
    n_pairs = N_DIFF_HEADS // 2
    kern = functools.partial(_diff_attn_kernel, tq=tq, tk=tk, post_scale=post_scale)
    return pl.pallas_call(
        kern,
        grid=(B, n_pairs, nq),
        in_specs=[
            pl.BlockSpec(memory_space=pltpu.SMEM),
            pl.BlockSpec(memory_space=pltpu.SMEM),
            pl.BlockSpec((tq, LANES), lambda b, p, i: (b * nq + i, p)),
            pl.BlockSpec((S, LANES), lambda b, p, i: (b, n_pairs + p)),
            pl.BlockSpec((S, LANES), lambda b, p, i: (b, 2 * n_pairs + p)),
            pl.BlockSpec((1, LANES), lambda b, p, i: (0, 0)),
        ],
        out_specs=pl.BlockSpec((tq, LANES), lambda b, p, i: (b * nq + i, p)),
        out_shape=jax.ShapeDtypeStruct((T, N_DIFF_HEADS * HEAD_DIM), jnp.bfloat16),
        scratch_shapes=[
            pltpu.VMEM((LANES, 4 * tq), jnp.bfloat16),
            pltpu.VMEM((S // tk, LANES, tk), jnp.bfloat16),
            pltpu.VMEM((2, tk, 4 * tq), jnp.float32),
            pltpu.VMEM((1, 4 * tq), jnp.float32),
            pltpu.VMEM((1, 4 * tq), jnp.float32),
            pltpu.VMEM((2, HEAD_DIM, 2 * tq), jnp.float32),
        ],
        compiler_params=_cparams(("arbitrary", "arbitrary", "arbitrary")),
        name="diff_attn",
    )(slopes, lam, proj, proj, proj, subln_w2)


def _dil_attn_kernel(slopes_ref, q_ref, k_ref, v_ref, o_ref,
                     qf_ref, kf_ref, vf_ref, qd_ref, kd_ref, vd_ref, a_ref,
                     po_ref, pl_ref, ro_ref, rl_ref, *, S):
    p = pl.program_id(1)
    W = DIL_W
    n_blocks = S // W
    qf_ref[...] = q_ref[...].astype(jnp.float32)
    kf_ref[...] = k_ref[...].astype(jnp.float32)
    vf_ref[...] = v_ref[...].astype(jnp.float32)
    kd_ref[0:W, :] = jnp.zeros((W, LANES), jnp.bfloat16)
    vd_ref[0:W, :] = jnp.zeros((W, LANES), jnp.bfloat16)
    lane = lax.broadcasted_iota(jnp.int32, (W, LANES), 1)
    first = lane < HEAD_DIM
    r_i = lax.broadcasted_iota(jnp.int32, (W, 2 * W), 0)
    c_i = lax.broadcasted_iota(jnp.int32, (W, 2 * W), 1)
    dist = W + r_i - c_i
    valid = (dist >= 0) & (dist <= W)
    dist_f = dist.astype(jnp.float32)

    for pi, (_, d) in enumerate(DIL_PATTERNS):
        L = S // d
        nbk = L // W
        if d == 1:
            qd_ref[...] = q_ref[...]
            kd_ref[W:W + S, :] = k_ref[...]
            vd_ref[W:W + S, :] = v_ref[...]
        else:
            for r in range(d):
                qd_ref[r * L:(r + 1) * L, :] = qf_ref[pl.ds(r, L, stride=d), :].astype(jnp.bfloat16)
                kd_ref[W + r * L:W + (r + 1) * L, :] = kf_ref[pl.ds(r, L, stride=d), :].astype(jnp.bfloat16)
                vd_ref[W + r * L:W + (r + 1) * L, :] = vf_ref[pl.ds(r, L, stride=d), :].astype(jnp.bfloat16)
        for h in range(2):
            slope = slopes_ref[2 * p + h] * float(d)
            full = jnp.where(valid, slope * dist_f, jnp.inf)
            a_ref[0, h * W:(h + 1) * W, :] = full
            a_ref[1, h * W:(h + 1) * W, :] = jnp.where(c_i < W, jnp.inf, full)

        def one_block(u, no_prev):
            row0 = pl.multiple_of(u * W, W)
            qb = qd_ref[pl.ds(row0, W), :].astype(jnp.float32)
            kk = kd_ref[pl.ds(row0, 2 * W), :]
            vv = vd_ref[pl.ds(row0, 2 * W), :]
            q2 = jnp.concatenate([jnp.where(first, qb, 0.0), jnp.where(first, 0.0, qb)],
                                 axis=0).astype(jnp.bfloat16)
            s = lax.dot_general(q2, kk, (((1,), (1,)), ((), ())),
                                preferred_element_type=jnp.float32)
            s = s - a_ref[no_prev]
            m = jnp.max(s, axis=1, keepdims=True)
            pr = jnp.exp2(s - m)
            l = jnp.sum(pr, axis=1, keepdims=True)
            pv = jnp.dot(pr.astype(jnp.bfloat16), vv, preferred_element_type=jnp.float32)
            o = pv / l
            lse = m + jnp.log2(l)
            po_ref[pl.ds(row0, W), :] = jnp.where(first, o[:W], o[W:])
            pl_ref[pl.ds(row0, W), :] = jnp.where(first, lse[:W], lse[W:])

        def group(g, carry, nbk=nbk):
            for i in range(DIL_UNROLL):
                u = g * DIL_UNROLL + i
                if nbk == 1:
                    no_prev = 1
                elif i > 0:
                    no_prev = 0
                elif nbk == DIL_UNROLL:
                    no_prev = 1
                else:
                    no_prev = (lax.rem(u, nbk) == 0).astype(jnp.int32)
                one_block(u, no_prev)
            return carry

        assert nbk == 1 or nbk % DIL_UNROLL == 0
        lax.fori_loop(0, n_blocks // DIL_UNROLL, group, 0)

        if d == 1:
            ro_ref[pi] = po_ref[...]
            rl_ref[pi] = pl_ref[...]
        else:
            for r in range(d):
                ro_ref[pi, pl.ds(r, L, stride=d), :] = po_ref[r * L:(r + 1) * L, :]
                rl_ref[pi, pl.ds(r, L, stride=d), :] = pl_ref[r * L:(r + 1) * L, :]

    def mix(c, carry):
        row0 = pl.multiple_of(c * W, W)
        ls = [rl_ref[i, pl.ds(row0, W), :] for i in range(3)]
        mx = jnp.maximum(jnp.maximum(ls[0], ls[1]), ls[2])
        es = [jnp.exp2(x - mx) for x in ls]
        den = es[0] + es[1] + es[2]
        num = (es[0] * ro_ref[0, pl.ds(row0, W), :] + es[1] * ro_ref[1, pl.ds(row0, W), :]
               + es[2] * ro_ref[2, pl.ds(row0, W), :])
        o_ref[pl.ds(row0, W), :] = (num / den).astype(o_ref.dtype)
        return carry

    lax.fori_loop(0, n_blocks, mix, 0)


def _dil_attn(proj, slopes, B, S):
    T = proj.shape[0]
    n_pairs = N_DIL_HEADS // 2
    base = 3 * N_DIFF_HEADS * HEAD_DIM // LANES
    W = DIL_W
    return pl.pallas_call(
        functools.partial(_dil_attn_kernel, S=S),
        grid=(B, n_pairs),
        in_specs=[
            pl.BlockSpec(memory_space=pltpu.SMEM),
            pl.BlockSpec((S, LANES), lambda b, p: (b, base + p)),
            pl.BlockSpec((S, LANES), lambda b, p: (b, base + n_pairs + p)),
            pl.BlockSpec((S, LANES), lambda b, p: (b, base + 2 * n_pairs + p)),
        ],
        out_specs=pl.BlockSpec((S, LANES), lambda b, p: (b, p)),
        out_shape=jax.ShapeDtypeStruct((T, N_DIL_HEADS * HEAD_DIM), jnp.bfloat16),
        scratch_shapes=[
            pltpu.VMEM((S, LANES), jnp.float32),
            pltpu.VMEM((S, LANES), jnp.float32),
            pltpu.VMEM((S, LANES), jnp.float32),
            pltpu.VMEM((S, LANES), jnp.bfloat16),
            pltpu.VMEM((S + W, LANES), jnp.bfloat16),
            pltpu.VMEM((S + W, LANES), jnp.bfloat16),
            pltpu.VMEM((2, 2 * W, 2 * W), jnp.float32),
            pltpu.VMEM((S, LANES), jnp.float32),
            pltpu.VMEM((S, LANES), jnp.float32),
            pltpu.VMEM((3, S, LANES), jnp.float32),
            pltpu.VMEM((3, S, LANES), jnp.float32),
        ],
        compiler_params=_cparams(("arbitrary", "arbitrary")),
        name="dil_attn",
    )(slopes, proj, proj, proj)


def _post_attn_kernel(x_ref, md_ref, ml_ref, wo_ref, nw_ref, rwh_ref, rwl_ref, rb_ref,
                      x1_ref, h2_ref, route_ref, gate_ref, cnt_ref, run_ref, *, tm):
    i = pl.program_id(0)

    @pl.when(i == 0)
    def _():
        run_ref[...] = jnp.zeros(run_ref.shape, jnp.float32)

    half = md_ref.shape[1]
    y = jnp.dot(md_ref[...], wo_ref[0:half, :], preferred_element_type=jnp.float32)
    y = y + jnp.dot(ml_ref[...], wo_ref[half:, :], preferred_element_type=jnp.float32)
    x1 = x_ref[...] + y
    x1_ref[...] = x1
    ms = jnp.mean(x1 * x1, axis=-1, keepdims=True)
    h2 = (x1 * lax.rsqrt(ms + NORM_EPS)) * nw_ref[...]
    h2_ref[...] = h2
    hh = h2.astype(jnp.bfloat16)
    hl = (h2 - hh.astype(jnp.float32)).astype(jnp.bfloat16)
    logits = jnp.dot(hh, rwh_ref[...], preferred_element_type=jnp.float32)
    logits = logits + jnp.dot(hl, rwh_ref[...], preferred_element_type=jnp.float32)
    logits = logits + jnp.dot(hh, rwl_ref[...], preferred_element_type=jnp.float32)
    logits = logits + rb_ref[...]
    lane = lax.broadcasted_iota(jnp.int32, (tm, LANES), 1)
    work = jnp.where(lane < N_EXPERTS, logits, NEG_INF)
    vals, idxs, sels = [], [], []
    for _ in range(TOP_K):
        mk = jnp.max(work, axis=1, keepdims=True)
        ik = jnp.min(jnp.where(work == mk, lane, LANES), axis=1, keepdims=True)
        sel = lane == ik
        work = jnp.where(sel, NEG_INF, work)
        vals.append(mk)
        idxs.append(ik)
        sels.append(sel)
    es = [jnp.exp(v - vals[0]) for v in vals]
    den = es[0] + es[1] + es[2] + es[3]
    member = jnp.zeros((tm, LANES), jnp.float32)
    for sel in sels:
        member = jnp.where(sel, 1.0, member)
    r_i = lax.broadcasted_iota(jnp.int32, (tm, tm), 0)
    c_i = lax.broadcasted_iota(jnp.int32, (tm, tm), 1)
    tri = jnp.where(c_i < r_i, 1.0, 0.0).astype(jnp.bfloat16)
    before = jnp.dot(tri, member.astype(jnp.bfloat16), preferred_element_type=jnp.float32)
    before = before + run_ref[...]
    route = jnp.zeros((tm, LANES), jnp.int32)
    gates = jnp.zeros((tm, LANES), jnp.float32)
    for k in range(TOP_K):
        rank = jnp.sum(jnp.where(sels[k], before, 0.0), axis=1, keepdims=True)
        route = jnp.where(lane == k, idxs[k], route)
        route = jnp.where(lane == TOP_K + k, rank.astype(jnp.int32), route)
        gates = jnp.where(lane == k, es[k] / den, gates)
    route_ref[...] = route
    gate_ref[...] = gates
    run_new = run_ref[...] + jnp.sum(member, axis=0, keepdims=True)
    run_ref[...] = run_new
    cnt_ref[...] = jnp.broadcast_to(run_new, cnt_ref.shape).astype(jnp.int32)


def _post_attn(x2d, mix_d, mix_l, wo_bf16, nw, rw_hi, rw_lo, rb, tm=512):
    T, D = x2d.shape
    half = mix_d.shape[1]
    return pl.pallas_call(
        functools.partial(_post_attn_kernel, tm=tm),
        grid=(T // tm,),
        in_specs=[
            pl.BlockSpec((tm, D), lambda i: (i, 0)),
            pl.BlockSpec((tm, half), lambda i: (i, 0)),
            pl.BlockSpec((tm, half), lambda i: (i, 0)),
            pl.BlockSpec((D, D), lambda i: (0, 0)),
            pl.BlockSpec((1, D), lambda i: (0, 0)),
            pl.BlockSpec((D, LANES), lambda i: (0, 0)),
            pl.BlockSpec((D, LANES), lambda i: (0, 0)),
            pl.BlockSpec((1, LANES), lambda i: (0, 0)),
        ],
        out_specs=[
            pl.BlockSpec((tm, D), lambda i: (i, 0)),
            pl.BlockSpec((tm, D), lambda i: (i, 0)),
            pl.BlockSpec((tm, LANES), lambda i: (i, 0)),
            pl.BlockSpec((tm, LANES), lambda i: (i, 0)),
            pl.BlockSpec((8, LANES), lambda i: (0, 0)),
        ],
        out_shape=[
            jax.ShapeDtypeStruct((T, D), jnp.float32),
            jax.ShapeDtypeStruct((T, D), jnp.float32),
            jax.ShapeDtypeStruct((T, LANES), jnp.int32),
            jax.ShapeDtypeStruct((T, LANES), jnp.float32),
            jax.ShapeDtypeStruct((8, LANES), jnp.int32),
        ],
        scratch_shapes=[pltpu.VMEM((1, LANES), jnp.float32)],
        compiler_params=_cparams(("arbitrary",)),
        name="post_attn",
    )(x2d, mix_d, mix_l, wo_bf16, nw, rw_hi, rw_lo, rb)


def _dispatch_kernel(dest_ref, cnt_ref, pcnt_ref, pst_ref, h_ref, xs_ref, z_ref, sem, zsem, *, td):
    i = pl.program_id(0)
    n = pl.num_programs(0)

    def row_copy(t, d):
        return pltpu.make_async_copy(h_ref.at[pl.ds(t, 1)], xs_ref.at[pl.ds(d, 1)], sem)

    def issue(t, carry):
        for k in range(TOP_K):
            row_copy(t, dest_ref[(i * td + t) * TOP_K + k]).start()
        return carry

    lax.fori_loop(0, td, issue, 0)

    def drain(t, carry):
        for k in range(TOP_K):
            row_copy(t, 0).wait()
        return carry

    lax.fori_loop(0, td, drain, 0)

    @pl.when(i == n - 1)
    def _():
        z_ref[...] = jnp.zeros(z_ref.shape, z_ref.dtype)

        def pad_copy(row):
            return pltpu.make_async_copy(z_ref.at[pl.ds(0, 1)], xs_ref.at[pl.ds(row, 1)], zsem)

        def per_expert(e, carry):
            base = pst_ref[e]

            def start(r, c):
                pad_copy(base + r).start()
                return c

            def wait(r, c):
                pad_copy(base + r).wait()
                return c

            lax.fori_loop(cnt_ref[e], pcnt_ref[e], start, 0)
            lax.fori_loop(cnt_ref[e], pcnt_ref[e], wait, 0)
            return carry

        lax.fori_loop(0, N_EXPERTS, per_expert, 0)

        def tail_copy(c):
            row = pl.multiple_of(c * td, td)
            return pltpu.make_async_copy(z_ref, xs_ref.at[pl.ds(row, td)], zsem)

        end = pst_ref[N_EXPERTS - 1] + pcnt_ref[N_EXPERTS - 1]
        first_chunk = end // td
        n_chunks = xs_ref.shape[0] // td

        def tail_start(c, carry):
            tail_copy(c).start()
            return carry

        def tail_wait(c, carry):
            tail_copy(c).wait()
            return carry

        lax.fori_loop(first_chunk, n_chunks, tail_start, 0)
        lax.fori_loop(first_chunk, n_chunks, tail_wait, 0)


def _dispatch(dest_flat, counts, pcounts, pstarts, h2, R, td=256):
    T, D = h2.shape
    return pl.pallas_call(
        functools.partial(_dispatch_kernel, td=td),
        grid_spec=pltpu.PrefetchScalarGridSpec(
            num_scalar_prefetch=4,
            grid=(T // td,),
            in_specs=[pl.BlockSpec((td, D), lambda i, *_: (i, 0))],
            out_specs=pl.BlockSpec(memory_space=pl.ANY),
            scratch_shapes=[
                pltpu.VMEM((td, D), h2.dtype),
                pltpu.SemaphoreType.DMA,
                pltpu.SemaphoreType.DMA,
            ],
        ),
        out_shape=jax.ShapeDtypeStruct((R, D), h2.dtype),
        compiler_params=_cparams(("arbitrary",)),
        name="dispatch",
    )(dest_flat, counts, pcounts, pstarts, h2)


def _experts_kernel(be_ref, used_ref, xs_ref, wgu_ref, bgu_ref, wd_ref, bd_ref, o_ref,
                    wgu_bf, wd_bf, *, n_chunk):
    i = pl.program_id(0)
    prev = be_ref[jnp.maximum(i - 1, 0)]
    new_expert = jnp.logical_or(i == 0, be_ref[i] != prev)

    @pl.when(new_expert)
    def _():
        wgu_bf[...] = wgu_ref[0].astype(jnp.bfloat16)
        wd_bf[...] = wd_ref[0].astype(jnp.bfloat16)

    @pl.when(i < used_ref[0])
    def _():
        xb = xs_ref[...].astype(jnp.bfloat16)
        acc = jnp.zeros(o_ref.shape, jnp.float32)
        for c in range(D_FF // n_chunk):
            gs = slice(c * n_chunk, (c + 1) * n_chunk)
            us = slice(D_FF + c * n_chunk, D_FF + (c + 1) * n_chunk)
            g = jnp.dot(xb, wgu_bf[:, gs], preferred_element_type=jnp.float32) + bgu_ref[0, :, gs]
            u = jnp.dot(xb, wgu_bf[:, us], preferred_element_type=jnp.float32) + bgu_ref[0, :, us]
            g = jnp.minimum(g, SWIGLU_LIMIT)
            u = jnp.clip(u, -SWIGLU_LIMIT, SWIGLU_LIMIT)
            act = (u + 1.0) * (g * jax.nn.sigmoid(SWIGLU_ALPHA * g))
            acc = acc + jnp.dot(act.astype(jnp.bfloat16), wd_bf[gs, :],
                                preferred_element_type=jnp.float32)
        o_ref[...] = acc + bd_ref[0]

    @pl.when(i >= used_ref[0])
    def _():
        o_ref[...] = jnp.zeros(o_ref.shape, o_ref.dtype)


def _experts(block_e, used, xs, w_gate_up, b_gate_up, w_down, b_down, blk=MOE_BLK):
    R, D = xs.shape
    n_blocks = R // blk
    F2 = w_gate_up.shape[2]

    def row_map(i, be, used):
        return (i, 0)

    return pl.pallas_call(
        functools.partial(_experts_kernel, n_chunk=512),
        grid_spec=pltpu.PrefetchScalarGridSpec(
            num_scalar_prefetch=2,
            grid=(n_blocks,),
            in_specs=[
                pl.BlockSpec((blk, D), row_map),
                pl.BlockSpec((1, D, F2), lambda i, be, used: (be[i], 0, 0)),
                pl.BlockSpec((1, 1, F2), lambda i, be, used: (be[i], 0, 0)),
                pl.BlockSpec((1, D_FF, D), lambda i, be, used: (be[i], 0, 0)),
                pl.BlockSpec((1, 1, D), lambda i, be, used: (be[i], 0, 0)),
            ],
            out_specs=pl.BlockSpec((blk, D), row_map),
            scratch_shapes=[
                pltpu.VMEM((D, F2), jnp.bfloat16),
                pltpu.VMEM((D_FF, D), jnp.bfloat16),
            ],
        ),
        out_shape=jax.ShapeDtypeStruct((R, D), jnp.float32),
        compiler_params=_cparams(("arbitrary",)),
        name="experts",
    )(block_e, used, xs, w_gate_up, b_gate_up.reshape(N_EXPERTS, 1, F2),
      w_down, b_down.reshape(N_EXPERTS, 1, D))


def _combine_kernel(dest_ref, x1_ref, g_ref, nw_ref, rows_ref, o_ref, buf_ref, sem, *, tc):
    i = pl.program_id(0)

    def row_copy(t, k, d):
        return pltpu.make_async_copy(rows_ref.at[pl.ds(d, 1)], buf_ref.at[k, pl.ds(t, 1)], sem)

    def issue(t, carry):
        for k in range(TOP_K):
            row_copy(t, k, dest_ref[(i * tc + t) * TOP_K + k]).start()
        return carry

    lax.fori_loop(0, tc, issue, 0)

    def drain(t, carry):
        for k in range(TOP_K):
            row_copy(t, k, 0).wait()
        return carry

    lax.fori_loop(0, tc, drain, 0)

    g = g_ref[...]
    y = x1_ref[...]
    for k in range(TOP_K):
        y = y + g[:, k:k + 1] * buf_ref[k]
    ms = jnp.mean(y * y, axis=-1, keepdims=True)
    o_ref[...] = (y * lax.rsqrt(ms + NORM_EPS)) * nw_ref[...]


def _combine(dest_flat, x1, gates, nw, rows, tc=128):
    T, D = x1.shape
    return pl.pallas_call(
        functools.partial(_combine_kernel, tc=tc),
        grid_spec=pltpu.PrefetchScalarGridSpec(
            num_scalar_prefetch=1,
            grid=(T // tc,),
            in_specs=[
                pl.BlockSpec((tc, D), lambda i, *_: (i, 0)),
                pl.BlockSpec((tc, LANES), lambda i, *_: (i, 0)),
                pl.BlockSpec((1, D), lambda i, *_: (0, 0)),
                pl.BlockSpec(memory_space=pl.ANY),
            ],
            out_specs=pl.BlockSpec((tc, D), lambda i, *_: (i, 0)),
            scratch_shapes=[
                pltpu.VMEM((TOP_K, tc, D), jnp.float32),
                pltpu.SemaphoreType.DMA,
            ],
        ),
        out_shape=jax.ShapeDtypeStruct((T, D), jnp.float32),
        compiler_params=_cparams(("arbitrary",)),
        name="combine",
    )(dest_flat, x1, gates, nw, rows)


def kernel(x, attn_norm_w, w_in, diff_lambda_q1, diff_lambda_k1, diff_lambda_q2, diff_lambda_k2,
           diff_subln_w, w_out, ffn_norm_w, router_w, router_b, w_gate_up, b_gate_up, w_down,
           b_down, final_norm_w):
    B, S, D = x.shape
    T = B * S
    f32 = jnp.float32
    n = jnp.arange(1, N_HEADS_TOTAL + 1, dtype=f32)
    slopes = jnp.exp2(-8.0 * n / N_HEADS_TOTAL)
    diff_slopes = slopes[0::2] * LOG2E
    dil_slopes = slopes[1::2] * LOG2E
    depth = attn_norm_w.shape[0]
    assert depth == 1, "the combine kernel fuses the final norm, so exactly one layer is supported"
    x2d = x.reshape(T, D)
    diff_w = N_DIFF_HEADS * HEAD_DIM
    colscale = jnp.concatenate([
        jnp.full((diff_w,), DIFF_QK_DIM ** -0.5 * LOG2E, f32), jnp.ones((2 * diff_w,), f32),
        jnp.full((N_DIL_HEADS * HEAD_DIM,), HEAD_DIM ** -0.5 * LOG2E, f32),
        jnp.ones((2 * N_DIL_HEADS * HEAD_DIM,), f32)]).reshape(1, IN_COLS)
    n_blocks = -(-(T * TOP_K + N_EXPERTS * (MOE_BLK - 1)) // MOE_BLK)
    R = n_blocks * MOE_BLK

    for l in range(depth):
        proj = _in_proj(x2d, attn_norm_w[l].reshape(1, D), w_in[l].astype(jnp.bfloat16), colscale)
        lam_init = 0.8 - 0.6 * math.exp(-0.3 * l)
        lam = (jnp.exp(jnp.sum(diff_lambda_q1[l] * diff_lambda_k1[l]).astype(f32))
               - jnp.exp(jnp.sum(diff_lambda_q2[l] * diff_lambda_k2[l]).astype(f32))
               + lam_init).reshape(1)
        subln2 = jnp.tile(diff_subln_w[l], 2).reshape(1, LANES)
        mix_d = _diff_attn(proj, diff_slopes, lam, subln2, B, S, 1.0 - lam_init)
        mix_l = _dil_attn(proj, dil_slopes, B, S)

        rw = jnp.zeros((D, LANES), f32).at[:, :N_EXPERTS].set(router_w[l])
        rw_hi = rw.astype(jnp.bfloat16)
        rw_lo = (rw - rw_hi.astype(f32)).astype(jnp.bfloat16)
        rb = jnp.zeros((1, LANES), f32).at[0, :N_EXPERTS].set(router_b[l])
        x1, h2, route, gates, cnt = _post_attn(
            x2d, mix_d, mix_l, w_out[l].astype(jnp.bfloat16), ffn_norm_w[l].reshape(1, D),
            rw_hi, rw_lo, rb)

        counts = cnt[0, :N_EXPERTS]
        pcounts = (counts + MOE_BLK - 1) // MOE_BLK * MOE_BLK
        pends = jnp.cumsum(pcounts)
        pstarts = pends - pcounts
        top_i = route[:, :TOP_K]
        rank = route[:, TOP_K:2 * TOP_K]
        onehot = top_i[..., None] == jnp.arange(N_EXPERTS, dtype=jnp.int32)
        dest = jnp.sum(jnp.where(onehot, pstarts.astype(jnp.int32), 0), axis=-1) + rank
        dest_flat = dest.reshape(T * TOP_K).astype(jnp.int32)
        used = (pends[-1] // MOE_BLK).astype(jnp.int32)
        blk_start = jnp.arange(n_blocks, dtype=jnp.int32) * MOE_BLK
        be = jnp.minimum(jnp.sum(pends[None, :] <= blk_start[:, None], axis=1), N_EXPERTS - 1)
        be_last = jnp.max(jnp.where(jnp.arange(n_blocks) < used, be, 0))
        block_e = jnp.where(jnp.arange(n_blocks) < used, be, be_last).astype(jnp.int32)

        xs = _dispatch(dest_flat, counts.astype(jnp.int32), pcounts.astype(jnp.int32),
                       pstarts.astype(jnp.int32), h2, R)
        rows = _experts(block_e, used.reshape(1), xs, w_gate_up[l], b_gate_up[l], w_down[l], b_down[l])
        x2d = _combine(dest_flat, x1, gates, final_norm_w.reshape(1, D), rows)
    return x2d.reshape(B, S, D)
```

```python
import functools
import math

import jax
import jax.numpy as jnp
from jax import lax
from jax.experimental import pallas as pl
from jax.experimental.pallas import tpu as pltpu

D_MODEL = 1024
HEAD_DIM = 64
N_DIFF_HEADS = 8
DIFF_QK_DIM = 32
N_DIL_HEADS = 8
N_HEADS_TOTAL = 16
DIL_PATTERNS = ((128, 1), (512, 4), (2048, 16))
DIL_W = 128
N_EXPERTS = 32
TOP_K = 4
D_FF = D_MODEL
SWIGLU_LIMIT = 7.0
SWIGLU_ALPHA = 1.702
NORM_EPS = 1e-5
IN_COLS = 3072
LANES = 128
DIL_UNROLL = 4
LOG2E = math.log2(math.e)
MOE_BLK = 512
VMEM_LIMIT = 56 * 1024 * 1024

NEG_INF = float("-inf")


def _cparams(sem, vmem=VMEM_LIMIT):
    return pltpu.CompilerParams(dimension_semantics=sem, vmem_limit_bytes=vmem)


def _in_proj_kernel(x_ref, nw_ref, w_ref, cs_ref, o_ref, *, n_chunk):
    xf = x_ref[...]
    ms = jnp.mean(xf * xf, axis=-1, keepdims=True)
    h = (xf * lax.rsqrt(ms + NORM_EPS)) * nw_ref[...]
    hb = h.astype(jnp.bfloat16)
    n_cols = o_ref.shape[1]
    for c in range(n_cols // n_chunk):
        sl = slice(c * n_chunk, (c + 1) * n_chunk)
        r = jnp.dot(hb, w_ref[:, sl], preferred_element_type=jnp.float32)
        o_ref[:, sl] = (r * cs_ref[:, sl]).astype(o_ref.dtype)


def _in_proj(x2d, nw, w_bf16, colscale, tm=512):
    T, D = x2d.shape
    N = w_bf16.shape[1]
    return pl.pallas_call(
        functools.partial(_in_proj_kernel, n_chunk=512),
        grid=(T // tm,),
        in_specs=[
            pl.BlockSpec((tm, D), lambda i: (i, 0)),
            pl.BlockSpec((1, D), lambda i: (0, 0)),
            pl.BlockSpec((D, N), lambda i: (0, 0)),
            pl.BlockSpec((1, N), lambda i: (0, 0)),
        ],
        out_specs=pl.BlockSpec((tm, N), lambda i: (i, 0)),
        out_shape=jax.ShapeDtypeStruct((T, N), jnp.bfloat16),
        compiler_params=_cparams(("arbitrary",)),
        name="in_proj",
    )(x2d, nw, w_bf16, colscale)


def _diff_attn_kernel(slopes_ref, lam_ref, q_ref, k_ref, v_ref, w_ref, o_ref,
                      q4t_ref, pos_ref, vt_ref, mask_ref, m_ref, acc_ref, *, tq, tk, post_scale):
    assert tq == tk
    p = pl.program_id(1)
    qi = pl.program_id(2)
    n_lanes = 4 * tq
    n_kv = v_ref.shape[0] // tk
    vt_rows = vt_ref.shape[2]

    lane = lax.broadcasted_iota(jnp.int32, (1, n_lanes), 1)
    slope_l = jnp.where(lane < 2 * tq, slopes_ref[2 * p], slopes_ref[2 * p + 1])

    @pl.when(qi == 0)
    def _():
        ones = jnp.ones((vt_rows - HEAD_DIM, tk), jnp.bfloat16)
        for jj in range(n_kv):
            vt = v_ref[jj * tk:(jj + 1) * tk, :].astype(jnp.float32).T.astype(jnp.bfloat16)
            for h in range(2):
                vt_ref[jj, h, 0:HEAD_DIM, :] = vt[h * HEAD_DIM:(h + 1) * HEAD_DIM, :]
                vt_ref[jj, h, HEAD_DIM:vt_rows, :] = ones
        hi = slope_l.astype(jnp.bfloat16).astype(jnp.float32)
        mid = (slope_l - hi).astype(jnp.bfloat16).astype(jnp.float32)
        lo = slope_l - hi - mid
        r16 = lax.broadcasted_iota(jnp.int32, (16, n_lanes), 0)
        rows = jnp.where(r16 == 0, hi, jnp.where(r16 == 1, mid, jnp.where(r16 == 2, lo, 0.0)))
        q4t_ref[LANES:LANES + 16, :] = rows.astype(jnp.bfloat16)
        q4t_ref[LANES + 16:, :] = jnp.zeros((LANES - 16, n_lanes), jnp.bfloat16)
        key_c = lax.broadcasted_iota(jnp.int32, (tk, LANES), 0)
        col_c = lax.broadcasted_iota(jnp.int32, (tk, LANES), 1)
        pos_ref[...] = jnp.where(col_c < 3, key_c, 0).astype(jnp.float32).astype(jnp.bfloat16)
        key_i = lax.broadcasted_iota(jnp.int32, (tk, n_lanes), 0)
        qry_i = lax.broadcasted_iota(jnp.int32, (tk, n_lanes), 1) & (tq - 1)
        mask_ref[...] = jnp.where(key_i <= qry_i, 0.0, NEG_INF)

    qt = q_ref[...].astype(jnp.float32).T
    sub = lax.broadcasted_iota(jnp.int32, (LANES, tq), 0) >> 5
    for c in range(4):
        q4t_ref[0:LANES, c * tq:(c + 1) * tq] = jnp.where(sub == c, qt, 0.0).astype(jnp.bfloat16)
    acc_ref[...] = jnp.zeros(acc_ref.shape, jnp.float32)

    def step(j, diag, m_prev):
        start = pl.multiple_of(j * tk, tk)
        ka = jnp.concatenate([k_ref[pl.ds(start, tk), :], pos_ref[...]], axis=1)
        s = jnp.dot(ka, q4t_ref[...], preferred_element_type=jnp.float32)
        if diag:
            s = s + mask_ref[...]
        off = slope_l * (j * tk).astype(jnp.float32)
        m_new = jnp.maximum(m_prev, jnp.max(s, axis=0, keepdims=True) + off)
        alpha = jnp.exp2(m_prev - m_new)
        pb = jnp.exp2(s - (m_new - off)).astype(jnp.bfloat16)
        for h in range(2):
            cols = slice(h * 2 * tq, (h + 1) * 2 * tq)
            pv = jnp.dot(vt_ref[j, h], pb[:, cols], preferred_element_type=jnp.float32)
            acc_ref[h] = alpha[:, cols] * acc_ref[h] + pv
        return m_new

    def pair(u, m):
        return step(2 * u + 1, False, step(2 * u, False, m))

    m_ref[...] = lax.fori_loop(0, qi // 2, pair, jnp.full((1, n_lanes), NEG_INF, jnp.float32))

    @pl.when((qi & 1) == 1)
    def _():
        step(qi, True, step(qi - 1, False, m_ref[...]))

    @pl.when((qi & 1) == 0)
    def _():
        step(qi, True, m_ref[...])

    lam = lam_ref[0]
    ys = []
    for h in range(2):
        acc = acc_ref[h]
        num = acc[0:HEAD_DIM]
        l = acc[HEAD_DIM:HEAD_DIM + 1]
        o = num[:, :tq] / l[:, :tq] - lam * (num[:, tq:] / l[:, tq:])
        ms = jnp.mean(o * o, axis=0, keepdims=True)
        ys.append(o * lax.rsqrt(ms + NORM_EPS))
    y = jnp.concatenate(ys, axis=0).T
    o_ref[...] = ((y * w_ref[...]) * post_scale).astype(o_ref.dtype)


def _diff_attn(proj, slopes, lam, subln_w2, B, S, post_scale, tq=256, tk=256):
    T = proj.shape[0]
    nq = S // tq
    n_pairs = N_DIFF_HEADS // 2
    kern = functools.partial(_diff_attn_kernel, tq=tq, tk=tk, post_scale=post_scale)
    return pl.pallas_call(
        kern,
        grid=(B, n_pairs, nq),
        in_specs=[
            pl.BlockSpec(memory_space=pltpu.SMEM),
            pl.BlockSpec(memory_space=pltpu.SMEM),
            pl.BlockSpec((tq, LANES), lambda b, p, i: (b * nq + i, p)),
            pl.BlockSpec((S, LANES), lambda b, p, i: (b, n_pairs + p)),
            pl.BlockSpec((S, LANES), lambda b, p, i: (b, 2 * n_pairs + p)),
            pl.BlockSpec((1, LANES), lambda b, p, i: (0, 0)),
        ],
        out_specs=pl.BlockSpec((tq, LANES), lambda b, p, i: (b * nq + i, p)),
        out_shape=jax.ShapeDtypeStruct((T, N_DIFF_HEADS * HEAD_DIM), jnp.bfloat16),
        scratch_shapes=[
            pltpu.VMEM((2 * LANES, 4 * tq), jnp.bfloat16),
            pltpu.VMEM((tk, LANES), jnp.bfloat16),
            pltpu.VMEM((S // tk, 2, HEAD_DIM + 16, tk), jnp.bfloat16),
            pltpu.VMEM((tk, 4 * tq), jnp.float32),
            pltpu.VMEM((1, 4 * tq), jnp.float32),
            pltpu.VMEM((2, HEAD_DIM + 16, 2 * tq), jnp.float32),
        ],
        compiler_params=_cparams(("arbitrary", "arbitrary", "arbitrary")),
        name="diff_attn",
    )(slopes, lam, proj, proj, proj, subln_w2)


def _dil_attn_kernel(slopes_ref, q_ref, k_ref, v_ref, o_ref,
                     qf_ref, kf_ref, vf_ref, qd_ref, kd_ref, vd_ref, a_ref,
                     po_ref, pl_ref, ro_ref, rl_ref, *, S):
    p = pl.program_id(1)
    W = DIL_W
    n_blocks = S // W
    qf_ref[...] = q_ref[...].astype(jnp.float32)
    kf_ref[...] = k_ref[...].astype(jnp.float32)
    vf_ref[...] = v_ref[...].astype(jnp.float32)
    kd_ref[0:W, :] = jnp.zeros((W, LANES), jnp.bfloat16)
    vd_ref[0:W, 0:LANES] = jnp.zeros((W, LANES), jnp.bfloat16)
    vd_ref[:, LANES:] = jnp.ones((S + W, LANES), jnp.bfloat16)
    lane = lax.broadcasted_iota(jnp.int32, (W, LANES), 1)
    first = lane < HEAD_DIM
    r_i = lax.broadcasted_iota(jnp.int32, (W, 2 * W), 0)
    c_i = lax.broadcasted_iota(jnp.int32, (W, 2 * W), 1)
    dist = W + r_i - c_i
    valid = (dist >= 0) & (dist <= W)
    dist_f = dist.astype(jnp.float32)

    for pi, (_, d) in enumerate(DIL_PATTERNS):
        L = S // d
        nbk = L // W
        if d == 1:
            qd_ref[...] = q_ref[...]
            kd_ref[W:W + S, :] = k_ref[...]
            vd_ref[W:W + S, 0:LANES] = v_ref[...]
        else:
            for r in range(d):
                qd_ref[r * L:(r + 1) * L, :] = qf_ref[pl.ds(r, L, stride=d), :].astype(jnp.bfloat16)
                kd_ref[W + r * L:W + (r + 1) * L, :] = kf_ref[pl.ds(r, L, stride=d), :].astype(jnp.bfloat16)
                vd_ref[W + r * L:W + (r + 1) * L, 0:LANES] = vf_ref[pl.ds(r, L, stride=d), :].astype(jnp.bfloat16)
        for h in range(2):
            slope = slopes_ref[2 * p + h] * float(d)
            full = jnp.where(valid, slope * dist_f, jnp.inf)
            a_ref[0, h * W:(h + 1) * W, :] = full
            a_ref[1, h * W:(h + 1) * W, :] = jnp.where(c_i < W, jnp.inf, full)

        def one_block(u, no_prev):
            row0 = pl.multiple_of(u * W, W)
            qb = qd_ref[pl.ds(row0, W), :].astype(jnp.float32)
            kk = kd_ref[pl.ds(row0, 2 * W), :]
            vv = vd_ref[pl.ds(row0, 2 * W), :]
            q2 = jnp.concatenate([jnp.where(first, qb, 0.0), jnp.where(first, 0.0, qb)],
                                 axis=0).astype(jnp.bfloat16)
            s = lax.dot_general(q2, kk, (((1,), (1,)), ((), ())),
                                preferred_element_type=jnp.float32)
            s = s - a_ref[no_prev]
            m = jnp.max(s, axis=1, keepdims=True)
            pr = jnp.exp2(s - m)
            pv = jnp.dot(pr.astype(jnp.bfloat16), vv, preferred_element_type=jnp.float32)
            l = pv[:, LANES:]
            o = pv[:, :LANES] / l
            lse = m + jnp.log2(l)
            po_ref[pl.ds(row0, W), :] = jnp.where(first, o[:W], o[W:])
            pl_ref[pl.ds(row0, W), :] = jnp.where(first, lse[:W], lse[W:])

        def group(g, carry, nbk=nbk):
            for i in range(DIL_UNROLL):
                u = g * DIL_UNROLL + i
                if nbk == 1:
                    no_prev = 1
                elif i > 0:
                    no_prev = 0
                elif nbk == DIL_UNROLL:
                    no_prev = 1
                else:
                    no_prev = (lax.rem(u, nbk) == 0).astype(jnp.int32)
                one_block(u, no_prev)
            return carry

        assert nbk == 1 or nbk % DIL_UNROLL == 0
        lax.fori_loop(0, n_blocks // DIL_UNROLL, group, 0)

        if d == 1:
            ro_ref[pi] = po_ref[...]
            rl_ref[pi] = pl_ref[...]
        else:
            for r in range(d):
                ro_ref[pi, pl.ds(r, L, stride=d), :] = po_ref[r * L:(r + 1) * L, :]
                rl_ref[pi, pl.ds(r, L, stride=d), :] = pl_ref[r * L:(r + 1) * L, :]

    def mix(c, carry):
        row0 = pl.multiple_of(c * W, W)
        ls = [rl_ref[i, pl.ds(row0, W), :] for i in range(3)]
        mx = jnp.maximum(jnp.maximum(ls[0], ls[1]), ls[2])
        es = [jnp.exp2(x - mx) for x in ls]
        den = es[0] + es[1] + es[2]
        num = (es[0] * ro_ref[0, pl.ds(row0, W), :] + es[1] * ro_ref[1, pl.ds(row0, W), :]
               + es[2] * ro_ref[2, pl.ds(row0, W), :])
        o_ref[pl.ds(row0, W), :] = (num / den).astype(o_ref.dtype)
        return carry

    lax.fori_loop(0, n_blocks, mix, 0)


def _dil_attn(proj, slopes, B, S):
    T = proj.shape[0]
    n_pairs = N_DIL_HEADS // 2
    base = 3 * N_DIFF_HEADS * HEAD_DIM // LANES
    W = DIL_W
    return pl.pallas_call(
        functools.partial(_dil_attn_kernel, S=S),
        grid=(B, n_pairs),
        in_specs=[
            pl.BlockSpec(memory_space=pltpu.SMEM),
            pl.BlockSpec((S, LANES), lambda b, p: (b, base + p)),
            pl.BlockSpec((S, LANES), lambda b, p: (b, base + n_pairs + p)),
            pl.BlockSpec((S, LANES), lambda b, p: (b, base + 2 * n_pairs + p)),
        ],
        out_specs=pl.BlockSpec((S, LANES), lambda b, p: (b, p)),
        out_shape=jax.ShapeDtypeStruct((T, N_DIL_HEADS * HEAD_DIM), jnp.bfloat16),
        scratch_shapes=[
            pltpu.VMEM((S, LANES), jnp.float32),
            pltpu.VMEM((S, LANES), jnp.float32),
            pltpu.VMEM((S, LANES), jnp.float32),
            pltpu.VMEM((S, LANES), jnp.bfloat16),
            pltpu.VMEM((S + W, LANES), jnp.bfloat16),
            pltpu.VMEM((S + W, 2 * LANES), jnp.bfloat16),
            pltpu.VMEM((2, 2 * W, 2 * W), jnp.float32),
            pltpu.VMEM((S, LANES), jnp.float32),
            pltpu.VMEM((S, LANES), jnp.float32),
            pltpu.VMEM((3, S, LANES), jnp.float32),
            pltpu.VMEM((3, S, LANES), jnp.float32),
        ],
        compiler_params=_cparams(("arbitrary", "arbitrary")),
        name="dil_attn",
    )(slopes, proj, proj, proj)


def _post_attn_kernel(x_ref, md_ref, ml_ref, wo_ref, nw_ref, rwh_ref, rwl_ref, rb_ref,
                      x1_ref, h2_ref, route_ref, gate_ref, cnt_ref, run_ref, *, tm):
    i = pl.program_id(0)

    @pl.when(i == 0)
    def _():
        run_ref[...] = jnp.zeros(run_ref.shape, jnp.float32)

    half = md_ref.shape[1]
    y = jnp.dot(md_ref[...], wo_ref[0:half, :], preferred_element_type=jnp.float32)
    y = y + jnp.dot(ml_ref[...], wo_ref[half:, :], preferred_element_type=jnp.float32)
    x1 = x_ref[...] + y
    x1_ref[...] = x1
    ms = jnp.mean(x1 * x1, axis=-1, keepdims=True)
    h2 = (x1 * lax.rsqrt(ms + NORM_EPS)) * nw_ref[...]
    h2_ref[...] = h2
    hh = h2.astype(jnp.bfloat16)
    hl = (h2 - hh.astype(jnp.float32)).astype(jnp.bfloat16)
    logits = jnp.dot(hh, rwh_ref[...], preferred_element_type=jnp.float32)
    logits = logits + jnp.dot(hl, rwh_ref[...], preferred_element_type=jnp.float32)
    logits = logits + jnp.dot(hh, rwl_ref[...], preferred_element_type=jnp.float32)
    logits = logits + rb_ref[...]
    lane = lax.broadcasted_iota(jnp.int32, (tm, LANES), 1)
    work = jnp.where(lane < N_EXPERTS, logits, NEG_INF)
    vals, idxs, sels = [], [], []
    for _ in range(TOP_K):
        mk = jnp.max(work, axis=1, keepdims=True)
        ik = jnp.min(jnp.where(work == mk, lane, LANES), axis=1, keepdims=True)
        sel = lane == ik
        work = jnp.where(sel, NEG_INF, work)
        vals.append(mk)
        idxs.append(ik)
        sels.append(sel)
    es = [jnp.exp(v - vals[0]) for v in vals]
    den = es[0] + es[1] + es[2] + es[3]
    member = jnp.zeros((tm, LANES), jnp.float32)
    for sel in sels:
        member = jnp.where(sel, 1.0, member)
    r_i = lax.broadcasted_iota(jnp.int32, (tm, tm), 0)
    c_i = lax.broadcasted_iota(jnp.int32, (tm, tm), 1)
    tri = jnp.where(c_i < r_i, 1.0, 0.0).astype(jnp.bfloat16)
    before = jnp.dot(tri, member.astype(jnp.bfloat16), preferred_element_type=jnp.float32)
    before = before + run_ref[...]
    route = jnp.zeros((tm, LANES), jnp.int32)
    gates = jnp.zeros((tm, LANES), jnp.float32)
    for k in range(TOP_K):
        rank = jnp.sum(jnp.where(sels[k], before, 0.0), axis=1, keepdims=True)
        route = jnp.where(lane == k, idxs[k], route)
        route = jnp.where(lane == TOP_K + k, rank.astype(jnp.int32), route)
        gates = jnp.where(lane == k, es[k] / den, gates)
    route_ref[...] = route
    gate_ref[...] = gates
    run_new = run_ref[...] + jnp.sum(member, axis=0, keepdims=True)
    run_ref[...] = run_new
    cnt_ref[...] = jnp.broadcast_to(run_new, cnt_ref.shape).astype(jnp.int32)


def _post_attn(x2d, mix_d, mix_l, wo_bf16, nw, rw_hi, rw_lo, rb, tm=512):
    T, D = x2d.shape
    half = mix_d.shape[1]
    return pl.pallas_call(
        functools.partial(_post_attn_kernel, tm=tm),
        grid=(T // tm,),
        in_specs=[
            pl.BlockSpec((tm, D), lambda i: (i, 0)),
            pl.BlockSpec((tm, half), lambda i: (i, 0)),
            pl.BlockSpec((tm, half), lambda i: (i, 0)),
            pl.BlockSpec((D, D), lambda i: (0, 0)),
            pl.BlockSpec((1, D), lambda i: (0, 0)),
            pl.BlockSpec((D, LANES), lambda i: (0, 0)),
            pl.BlockSpec((D, LANES), lambda i: (0, 0)),
            pl.BlockSpec((1, LANES), lambda i: (0, 0)),
        ],
        out_specs=[
            pl.BlockSpec((tm, D), lambda i: (i, 0)),
            pl.BlockSpec((tm, D), lambda i: (i, 0)),
            pl.BlockSpec((tm, LANES), lambda i: (i, 0)),
            pl.BlockSpec((tm, LANES), lambda i: (i, 0)),
            pl.BlockSpec((8, LANES), lambda i: (0, 0)),
        ],
        out_shape=[
            jax.ShapeDtypeStruct((T, D), jnp.float32),
            jax.ShapeDtypeStruct((T, D), jnp.float32),
            jax.ShapeDtypeStruct((T, LANES), jnp.int32),
            jax.ShapeDtypeStruct((T, LANES), jnp.float32),
            jax.ShapeDtypeStruct((8, LANES), jnp.int32),
        ],
        scratch_shapes=[pltpu.VMEM((1, LANES), jnp.float32)],
        compiler_params=_cparams(("arbitrary",)),
        name="post_attn",
    )(x2d, mix_d, mix_l, wo_bf16, nw, rw_hi, rw_lo, rb)


def _dispatch_kernel(dest_ref, cnt_ref, pcnt_ref, pst_ref, h_ref, xs_ref, z_ref, sem, zsem, *, td):
    i = pl.program_id(0)
    n = pl.num_programs(0)

    def row_copy(t, d):
        return pltpu.make_async_copy(h_ref.at[pl.ds(t, 1)], xs_ref.at[pl.ds(d, 1)], sem)

    def issue(t, carry):
        for k in range(TOP_K):
            row_copy(t, dest_ref[(i * td + t) * TOP_K + k]).start(priority=k % 2)
        return carry

    lax.fori_loop(0, td, issue, 0)

    for k in range(TOP_K):
        pltpu.make_async_copy(h_ref, xs_ref.at[pl.ds(0, td)], sem).wait()

    @pl.when(i == n - 1)
    def _():
        z_ref[...] = jnp.zeros(z_ref.shape, z_ref.dtype)

        def pad_copy(row):
            return pltpu.make_async_copy(z_ref.at[pl.ds(0, 1)], xs_ref.at[pl.ds(row, 1)], zsem)

        def per_expert(e, carry):
            base = pst_ref[e]

            def start(r, c):
                pad_copy(base + r).start()
                return c

            def wait(r, c):
                pad_copy(base + r).wait()
                return c

            lax.fori_loop(cnt_ref[e], pcnt_ref[e], start, 0)
            lax.fori_loop(cnt_ref[e], pcnt_ref[e], wait, 0)
            return carry

        lax.fori_loop(0, N_EXPERTS, per_expert, 0)

        def tail_copy(c):
            row = pl.multiple_of(c * td, td)
            return pltpu.make_async_copy(z_ref, xs_ref.at[pl.ds(row, td)], zsem)

        end = pst_ref[N_EXPERTS - 1] + pcnt_ref[N_EXPERTS - 1]
        first_chunk = end // td
        n_chunks = xs_ref.shape[0] // td

        def tail_start(c, carry):
            tail_copy(c).start()
            return carry

        def tail_wait(c, carry):
            tail_copy(c).wait()
            return carry

        lax.fori_loop(first_chunk, n_chunks, tail_start, 0)
        lax.fori_loop(first_chunk, n_chunks, tail_wait, 0)


def _dispatch(dest_flat, counts, pcounts, pstarts, h2, R, td=256):
    T, D = h2.shape
    return pl.pallas_call(
        functools.partial(_dispatch_kernel, td=td),
        grid_spec=pltpu.PrefetchScalarGridSpec(
            num_scalar_prefetch=4,
            grid=(T // td,),
            in_specs=[pl.BlockSpec((td, D), lambda i, *_: (i, 0))],
            out_specs=pl.BlockSpec(memory_space=pl.ANY),
            scratch_shapes=[
                pltpu.VMEM((td, D), h2.dtype),
                pltpu.SemaphoreType.DMA,
                pltpu.SemaphoreType.DMA,
            ],
        ),
        out_shape=jax.ShapeDtypeStruct((R, D), h2.dtype),
        compiler_params=_cparams(("arbitrary",)),
        name="dispatch",
    )(dest_flat, counts, pcounts, pstarts, h2)


def _experts_kernel(be_ref, used_ref, xs_ref, wgu_ref, bgu_ref, wd_ref, bd_ref, o_ref,
                    wgu_bf, wd_bf, *, n_chunk):
    i = pl.program_id(0)
    prev = be_ref[jnp.maximum(i - 1, 0)]
    new_expert = jnp.logical_or(i == 0, be_ref[i] != prev)

    @pl.when(new_expert)
    def _():
        wgu_bf[...] = wgu_ref[0].astype(jnp.bfloat16)
        wd_bf[...] = wd_ref[0].astype(jnp.bfloat16)

    @pl.when(i < used_ref[0])
    def _():
        xb = xs_ref[...].astype(jnp.bfloat16)
        acc = jnp.zeros(o_ref.shape, jnp.float32)
        for c in range(D_FF // n_chunk):
            gs = slice(c * n_chunk, (c + 1) * n_chunk)
            us = slice(D_FF + c * n_chunk, D_FF + (c + 1) * n_chunk)
            g = jnp.dot(xb, wgu_bf[:, gs], preferred_element_type=jnp.float32) + bgu_ref[0, :, gs]
            u = jnp.dot(xb, wgu_bf[:, us], preferred_element_type=jnp.float32) + bgu_ref[0, :, us]
            g = jnp.minimum(g, SWIGLU_LIMIT)
            u = jnp.clip(u, -SWIGLU_LIMIT, SWIGLU_LIMIT)
            act = (u + 1.0) * (g * jax.nn.sigmoid(SWIGLU_ALPHA * g))
            acc = acc + jnp.dot(act.astype(jnp.bfloat16), wd_bf[gs, :],
                                preferred_element_type=jnp.float32)
        o_ref[...] = acc + bd_ref[0]

    @pl.when(i >= used_ref[0])
    def _():
        o_ref[...] = jnp.zeros(o_ref.shape, o_ref.dtype)


def _experts(block_e, used, xs, w_gate_up, b_gate_up, w_down, b_down, blk=MOE_BLK):
    R, D = xs.shape
    n_blocks = R // blk
    F2 = w_gate_up.shape[2]

    def row_map(i, be, used):
        return (i, 0)

    return pl.pallas_call(
        functools.partial(_experts_kernel, n_chunk=512),
        grid_spec=pltpu.PrefetchScalarGridSpec(
            num_scalar_prefetch=2,
            grid=(n_blocks,),
            in_specs=[
                pl.BlockSpec((blk, D), row_map),
                pl.BlockSpec((1, D, F2), lambda i, be, used: (be[i], 0, 0)),
                pl.BlockSpec((1, 1, F2), lambda i, be, used: (be[i], 0, 0)),
                pl.BlockSpec((1, D_FF, D), lambda i, be, used: (be[i], 0, 0)),
                pl.BlockSpec((1, 1, D), lambda i, be, used: (be[i], 0, 0)),
            ],
            out_specs=pl.BlockSpec((blk, D), row_map),
            scratch_shapes=[
                pltpu.VMEM((D, F2), jnp.bfloat16),
                pltpu.VMEM((D_FF, D), jnp.bfloat16),
            ],
        ),
        out_shape=jax.ShapeDtypeStruct((R, D), jnp.float32),
        compiler_params=_cparams(("arbitrary",)),
        name="experts",
    )(block_e, used, xs, w_gate_up, b_gate_up.reshape(N_EXPERTS, 1, F2),
      w_down, b_down.reshape(N_EXPERTS, 1, D))


def _combine_kernel(dest_ref, x1_ref, g_ref, nw_ref, rows_ref, o_ref, buf_ref, sem, *, tc):
    i = pl.program_id(0)

    def row_copy(t, k, d):
        return pltpu.make_async_copy(rows_ref.at[pl.ds(d, 1)], buf_ref.at[k, pl.ds(t, 1)], sem)

    def issue(t, carry):
        for k in range(TOP_K):
            row_copy(t, k, dest_ref[(i * tc + t) * TOP_K + k]).start(priority=k % 2)
        return carry

    lax.fori_loop(0, tc, issue, 0)

    for k in range(TOP_K):
        pltpu.make_async_copy(rows_ref.at[pl.ds(0, tc)], buf_ref.at[k], sem).wait()

    g = g_ref[...]
    y = x1_ref[...]
    for k in range(TOP_K):
        y = y + g[:, k:k + 1] * buf_ref[k]
    ms = jnp.mean(y * y, axis=-1, keepdims=True)
    o_ref[...] = (y * lax.rsqrt(ms + NORM_EPS)) * nw_ref[...]


def _combine(dest_flat, x1, gates, nw, rows, tc=128):
    T, D = x1.shape
    return pl.pallas_call(
        functools.partial(_combine_kernel, tc=tc),
        grid_spec=pltpu.PrefetchScalarGridSpec(
            num_scalar_prefetch=1,
            grid=(T // tc,),
            in_specs=[
                pl.BlockSpec((tc, D), lambda i, *_: (i, 0)),
                pl.BlockSpec((tc, LANES), lambda i, *_: (i, 0)),
                pl.BlockSpec((1, D), lambda i, *_: (0, 0)),
                pl.BlockSpec(memory_space=pl.ANY),
            ],
            out_specs=pl.BlockSpec((tc, D), lambda i, *_: (i, 0)),
            scratch_shapes=[
                pltpu.VMEM((TOP_K, tc, D), jnp.float32),
                pltpu.SemaphoreType.DMA,
            ],
        ),
        out_shape=jax.ShapeDtypeStruct((T, D), jnp.float32),
        compiler_params=_cparams(("arbitrary",)),
        name="combine",
    )(dest_flat, x1, gates, nw, rows)


def kernel(x, attn_norm_w, w_in, diff_lambda_q1, diff_lambda_k1, diff_lambda_q2, diff_lambda_k2,
           diff_subln_w, w_out, ffn_norm_w, router_w, router_b, w_gate_up, b_gate_up, w_down,
           b_down, final_norm_w):
    B, S, D = x.shape
    T = B * S
    f32 = jnp.float32
    n = jnp.arange(1, N_HEADS_TOTAL + 1, dtype=f32)
    slopes = jnp.exp2(-8.0 * n / N_HEADS_TOTAL)
    diff_slopes = slopes[0::2] * LOG2E
    dil_slopes = slopes[1::2] * LOG2E
    depth = attn_norm_w.shape[0]
    assert depth == 1, "the combine kernel fuses the final norm, so exactly one layer is supported"
    x2d = x.reshape(T, D)
    diff_w = N_DIFF_HEADS * HEAD_DIM
    colscale = jnp.concatenate([
        jnp.full((diff_w,), DIFF_QK_DIM ** -0.5 * LOG2E, f32), jnp.ones((2 * diff_w,), f32),
        jnp.full((N_DIL_HEADS * HEAD_DIM,), HEAD_DIM ** -0.5 * LOG2E, f32),
        jnp.ones((2 * N_DIL_HEADS * HEAD_DIM,), f32)]).reshape(1, IN_COLS)
    n_blocks = -(-(T * TOP_K + N_EXPERTS * (MOE_BLK - 1)) // MOE_BLK)
    R = n_blocks * MOE_BLK

    for l in range(depth):
        proj = _in_proj(x2d, attn_norm_w[l].reshape(1, D), w_in[l].astype(jnp.bfloat16), colscale)
        lam_init = 0.8 - 0.6 * math.exp(-0.3 * l)
        lam = (jnp.exp(jnp.sum(diff_lambda_q1[l] * diff_lambda_k1[l]).astype(f32))
               - jnp.exp(jnp.sum(diff_lambda_q2[l] * diff_lambda_k2[l]).astype(f32))
               + lam_init).reshape(1)
        subln2 = jnp.tile(diff_subln_w[l], 2).reshape(1, LANES)
        mix_d = _diff_attn(proj, diff_slopes, lam, subln2, B, S, 1.0 - lam_init)
        mix_l = _dil_attn(proj, dil_slopes, B, S)

        rw = jnp.zeros((D, LANES), f32).at[:, :N_EXPERTS].set(router_w[l])
        rw_hi = rw.astype(jnp.bfloat16)
        rw_lo = (rw - rw_hi.astype(f32)).astype(jnp.bfloat16)
        rb = jnp.zeros((1, LANES), f32).at[0, :N_EXPERTS].set(router_b[l])
        x1, h2, route, gates, cnt = _post_attn(
            x2d, mix_d, mix_l, w_out[l].astype(jnp.bfloat16), ffn_norm_w[l].reshape(1, D),
            rw_hi, rw_lo, rb)

        counts = cnt[0, :N_EXPERTS]
        pcounts = (counts + MOE_BLK - 1) // MOE_BLK * MOE_BLK
        pends = jnp.cumsum(pcounts)
        pstarts = pends - pcounts
        top_i = route[:, :TOP_K]
        rank = route[:, TOP_K:2 * TOP_K]
        onehot = top_i[..., None] == jnp.arange(N_EXPERTS, dtype=jnp.int32)
        dest = jnp.sum(jnp.where(onehot, pstarts.astype(jnp.int32), 0), axis=-1) + rank
        dest_flat = dest.reshape(T * TOP_K).astype(jnp.int32)
        used = (pends[-1] // MOE_BLK).astype(jnp.int32)
        blk_start = jnp.arange(n_blocks, dtype=jnp.int32) * MOE_BLK
        be = jnp.minimum(jnp.sum(pends[None, :] <= blk_start[:, None], axis=1), N_EXPERTS - 1)
        be_last = jnp.max(jnp.where(jnp.arange(n_blocks) < used, be, 0))
        block_e = jnp.where(jnp.arange(n_blocks) < used, be, be_last).astype(jnp.int32)

        xs = _dispatch(dest_flat, counts.astype(jnp.int32), pcounts.astype(jnp.int32),
                       pstarts.astype(jnp.int32), h2, R)
        rows = _experts(block_e, used.reshape(1), xs, w_gate_up[l], b_gate_up[l], w_down[l], b_down[l])
        x2d = _combine(dest_flat, x1, gates, final_norm_w.reshape(1, D), rows)
    return x2d.reshape(B, S, D)
```

```python
import functools
import math

import jax
import jax.numpy as jnp
from jax import lax
from jax.experimental import pallas as pl
from jax.experimental.pallas import tpu as pltpu

D_MODEL = 1024
HEAD_DIM = 64
N_DIFF_HEADS = 8
DIFF_QK_DIM = 32
N_DIL_HEADS = 8
N_HEADS_TOTAL = 16
DIL_PATTERNS = ((128, 1), (512, 4), (2048, 16))
DIL_W = 128
N_EXPERTS = 32
TOP_K = 4
D_FF = D_MODEL
SWIGLU_LIMIT = 7.0
SWIGLU_ALPHA = 1.702
NORM_EPS = 1e-5
IN_COLS = 3072
LANES = 128
DIL_UNROLL = 4
LOG2E = math.log2(math.e)
MOE_BLK = 512
VMEM_LIMIT = 56 * 1024 * 1024

NEG_INF = float("-inf")


def _cparams(sem, vmem=VMEM_LIMIT):
    return pltpu.CompilerParams(dimension_semantics=sem, vmem_limit_bytes=vmem)


def _in_proj_kernel(x_ref, nw_ref, w_ref, cs_ref, o_ref, *, n_chunk):
    xf = x_ref[...]
    ms = jnp.mean(xf * xf, axis=-1, keepdims=True)
    h = (xf * lax.rsqrt(ms + NORM_EPS)) * nw_ref[...]
    hb = h.astype(jnp.bfloat16)
    n_cols = o_ref.shape[1]
    for c in range(n_cols // n_chunk):
        sl = slice(c * n_chunk, (c + 1) * n_chunk)
        r = jnp.dot(hb, w_ref[:, sl], preferred_element_type=jnp.float32)
        o_ref[:, sl] = (r * cs_ref[:, sl]).astype(o_ref.dtype)


def _in_proj(x2d, nw, w_bf16, colscale, tm=512):
    T, D = x2d.shape
    N = w_bf16.shape[1]
    return pl.pallas_call(
        functools.partial(_in_proj_kernel, n_chunk=512),
        grid=(T // tm,),
        in_specs=[
            pl.BlockSpec((tm, D), lambda i: (i, 0)),
            pl.BlockSpec((1, D), lambda i: (0, 0)),
            pl.BlockSpec((D, N), lambda i: (0, 0)),
            pl.BlockSpec((1, N), lambda i: (0, 0)),
        ],
        out_specs=pl.BlockSpec((tm, N), lambda i: (i, 0)),
        out_shape=jax.ShapeDtypeStruct((T, N), jnp.bfloat16),
        compiler_params=_cparams(("arbitrary",)),
        name="in_proj",
    )(x2d, nw, w_bf16, colscale)


def _diff_attn_kernel(slopes_ref, lam_ref, q_ref, k_ref, v_ref, w_ref, o_ref,
                      q4t_ref, pos_ref, vt_ref, mask_ref, s_ref, mt_ref, m_ref, acc_ref,
                      *, tq, tk, post_scale):
    assert tq == tk
    p = pl.program_id(1)
    qi = pl.program_id(2)
    n_lanes = 4 * tq
    n_kv = v_ref.shape[0] // tk
    vt_rows = vt_ref.shape[2]

    lane = lax.broadcasted_iota(jnp.int32, (1, n_lanes), 1)
    slope_l = jnp.where(lane < 2 * tq, slopes_ref[2 * p], slopes_ref[2 * p + 1])

    @pl.when(qi == 0)
    def _():
        ones = jnp.ones((vt_rows - HEAD_DIM, tk), jnp.bfloat16)
        for jj in range(n_kv):
            vt = v_ref[jj * tk:(jj + 1) * tk, :].astype(jnp.float32).T.astype(jnp.bfloat16)
            for h in range(2):
                vt_ref[jj, h, 0:HEAD_DIM, :] = vt[h * HEAD_DIM:(h + 1) * HEAD_DIM, :]
                vt_ref[jj, h, HEAD_DIM:vt_rows, :] = ones
        hi = slope_l.astype(jnp.bfloat16).astype(jnp.float32)
        mid = (slope_l - hi).astype(jnp.bfloat16).astype(jnp.float32)
        lo = slope_l - hi - mid
        r16 = lax.broadcasted_iota(jnp.int32, (16, n_lanes), 0)
        rows = jnp.where(r16 == 0, hi, jnp.where(r16 == 1, mid, jnp.where(r16 == 2, lo, 0.0)))
        q4t_ref[LANES:LANES + 16, :] = rows.astype(jnp.bfloat16)
        q4t_ref[LANES + 16:, :] = jnp.zeros((LANES - 16, n_lanes), jnp.bfloat16)
        key_c = lax.broadcasted_iota(jnp.int32, (tk, LANES), 0)
        col_c = lax.broadcasted_iota(jnp.int32, (tk, LANES), 1)
        pos_ref[...] = jnp.where(col_c < 3, key_c, 0).astype(jnp.float32).astype(jnp.bfloat16)
        key_i = lax.broadcasted_iota(jnp.int32, (tk, n_lanes), 0)
        qry_i = lax.broadcasted_iota(jnp.int32, (tk, n_lanes), 1) & (tq - 1)
        mask_ref[0] = jnp.zeros((tk, n_lanes), jnp.float32)
        mask_ref[1] = jnp.where(key_i <= qry_i, 0.0, NEG_INF)

    qt = q_ref[...].astype(jnp.float32).T
    sub = lax.broadcasted_iota(jnp.int32, (LANES, tq), 0) >> 5
    for c in range(4):
        q4t_ref[0:LANES, c * tq:(c + 1) * tq] = jnp.where(sub == c, qt, 0.0).astype(jnp.bfloat16)
    acc_ref[...] = jnp.zeros(acc_ref.shape, jnp.float32)

    def tile_off(n):
        return slope_l * jnp.asarray(n * tk).astype(jnp.float32)

    def scores(n, slot, mask):
        start = pl.multiple_of(n * tk, tk)
        ka = jnp.concatenate([k_ref[pl.ds(start, tk), :], pos_ref[...]], axis=1)
        s = jnp.dot(ka, q4t_ref[...], preferred_element_type=jnp.float32)
        if mask == "diag":
            s = s + mask_ref[1]
        elif mask == "maybe":
            s = s + mask_ref[(n == qi).astype(jnp.int32)]
        s_ref[slot] = s
        mt_ref[slot] = jnp.max(s, axis=0, keepdims=True) + tile_off(n)

    def weights(n, slot, m_prev):
        m_new = jnp.maximum(m_prev, mt_ref[slot])
        alpha = jnp.exp2(m_prev - m_new)
        pb = jnp.exp2(s_ref[slot] - (m_new - tile_off(n))).astype(jnp.bfloat16)
        for h in range(2):
            cols = slice(h * 2 * tq, (h + 1) * 2 * tq)
            pv = jnp.dot(vt_ref[n, h], pb[:, cols], preferred_element_type=jnp.float32)
            acc_ref[h] = alpha[:, cols] * acc_ref[h] + pv
        return m_new

    scores(0, 0, "maybe")

    def pair(u, m):
        scores(2 * u + 1, 1, None)
        m = weights(2 * u, 0, m)
        scores(2 * u + 2, 0, "maybe")
        return weights(2 * u + 1, 1, m)

    m_ref[...] = lax.fori_loop(0, qi // 2, pair, jnp.full((1, n_lanes), NEG_INF, jnp.float32))

    @pl.when((qi & 1) == 1)
    def _():
        scores(qi, 1, "diag")
        weights(qi, 1, weights(qi - 1, 0, m_ref[...]))

    @pl.when((qi & 1) == 0)
    def _():
        weights(qi, 0, m_ref[...])

    lam = lam_ref[0]
    ys = []
    for h in range(2):
        acc = acc_ref[h]
        num = acc[0:HEAD_DIM]
        l = acc[HEAD_DIM:HEAD_DIM + 1]
        o = num[:, :tq] / l[:, :tq] - lam * (num[:, tq:] / l[:, tq:])
        ms = jnp.mean(o * o, axis=0, keepdims=True)
        ys.append(o * lax.rsqrt(ms + NORM_EPS))
    y = jnp.concatenate(ys, axis=0).T
    o_ref[...] = ((y * w_ref[...]) * post_scale).astype(o_ref.dtype)


def _diff_attn(proj, slopes, lam, subln_w2, B, S, post_scale, tq=256, tk=256):
    T = proj.shape[0]
    nq = S // tq
    n_pairs = N_DIFF_HEADS // 2
    kern = functools.partial(_diff_attn_kernel, tq=tq, tk=tk, post_scale=post_scale)
    return pl.pallas_call(
        kern,
        grid=(B, n_pairs, nq),
        in_specs=[
            pl.BlockSpec(memory_space=pltpu.SMEM),
            pl.BlockSpec(memory_space=pltpu.SMEM),
            pl.BlockSpec((tq, LANES), lambda b, p, i: (b * nq + i, p)),
            pl.BlockSpec((S, LANES), lambda b, p, i: (b, n_pairs + p)),
            pl.BlockSpec((S, LANES), lambda b, p, i: (b, 2 * n_pairs + p)),
            pl.BlockSpec((1, LANES), lambda b, p, i: (0, 0)),
        ],
        out_specs=pl.BlockSpec((tq, LANES), lambda b, p, i: (b * nq + i, p)),
        out_shape=jax.ShapeDtypeStruct((T, N_DIFF_HEADS * HEAD_DIM), jnp.bfloat16),
        scratch_shapes=[
            pltpu.VMEM((2 * LANES, 4 * tq), jnp.bfloat16),
            pltpu.VMEM((tk, LANES), jnp.bfloat16),
            pltpu.VMEM((S // tk, 2, HEAD_DIM + 16, tk), jnp.bfloat16),
            pltpu.VMEM((2, tk, 4 * tq), jnp.float32),
            pltpu.VMEM((2, tk, 4 * tq), jnp.float32),
            pltpu.VMEM((2, 1, 4 * tq), jnp.float32),
            pltpu.VMEM((1, 4 * tq), jnp.float32),
            pltpu.VMEM((2, HEAD_DIM + 16, 2 * tq), jnp.float32),
        ],
        compiler_params=_cparams(("arbitrary", "arbitrary", "arbitrary")),
        name="diff_attn",
    )(slopes, lam, proj, proj, proj, subln_w2)


def _dil_attn_kernel(slopes_ref, q_ref, k_ref, v_ref, o_ref,
                     qf_ref, kf_ref, vf_ref, qd_ref, kd_ref, vd_ref, a_ref,
                     po_ref, pl_ref, ro_ref, rl_ref, *, S):
    p = pl.program_id(1)
    W = DIL_W
    n_blocks = S // W
    qf_ref[...] = q_ref[...].astype(jnp.float32)
    kf_ref[...] = k_ref[...].astype(jnp.float32)
    vf_ref[...] = v_ref[...].astype(jnp.float32)
    @pl.when((pl.program_id(0) == 0) & (p == 0))
    def _():
        kd_ref[0:W, :] = jnp.zeros((W, LANES), jnp.bfloat16)
        vd_ref[0:W, 0:LANES] = jnp.zeros((W, LANES), jnp.bfloat16)
        vd_ref[:, LANES:] = jnp.ones((S + W, LANES), jnp.bfloat16)

    lane = lax.broadcasted_iota(jnp.int32, (W, LANES), 1)
    first = lane < HEAD_DIM
    r_i = lax.broadcasted_iota(jnp.int32, (W, 2 * W), 0)
    c_i = lax.broadcasted_iota(jnp.int32, (W, 2 * W), 1)
    dist = W + r_i - c_i
    valid = (dist >= 0) & (dist <= W)
    dist_f = dist.astype(jnp.float32)

    for pi, (_, d) in enumerate(DIL_PATTERNS):
        L = S // d
        nbk = L // W
        if d == 1:
            qd_ref[...] = q_ref[...]
            kd_ref[W:W + S, :] = k_ref[...]
            vd_ref[W:W + S, 0:LANES] = v_ref[...]
        else:
            for r in range(d):
                qd_ref[r * L:(r + 1) * L, :] = qf_ref[pl.ds(r, L, stride=d), :].astype(jnp.bfloat16)
                kd_ref[W + r * L:W + (r + 1) * L, :] = kf_ref[pl.ds(r, L, stride=d), :].astype(jnp.bfloat16)
                vd_ref[W + r * L:W + (r + 1) * L, 0:LANES] = vf_ref[pl.ds(r, L, stride=d), :].astype(jnp.bfloat16)
        for h in range(2):
            slope = slopes_ref[2 * p + h] * float(d)
            full = jnp.where(valid, slope * dist_f, jnp.inf)
            a_ref[0, h * W:(h + 1) * W, :] = full
            a_ref[1, h * W:(h + 1) * W, :] = jnp.where(c_i < W, jnp.inf, full)

        dst_o = ro_ref.at[pi] if d == 1 else po_ref
        dst_l = rl_ref.at[pi] if d == 1 else pl_ref

        def one_block(u, no_prev, dst_o=dst_o, dst_l=dst_l):
            row0 = pl.multiple_of(u * W, W)
            qb = qd_ref[pl.ds(row0, W), :].astype(jnp.float32)
            kk = kd_ref[pl.ds(row0, 2 * W), :]
            vv = vd_ref[pl.ds(row0, 2 * W), :]
            q2 = jnp.concatenate([jnp.where(first, qb, 0.0), jnp.where(first, 0.0, qb)],
                                 axis=0).astype(jnp.bfloat16)
            s = lax.dot_general(q2, kk, (((1,), (1,)), ((), ())),
                                preferred_element_type=jnp.float32)
            s = s - a_ref[no_prev]
            m = jnp.max(s, axis=1, keepdims=True)
            pr = jnp.exp2(s - m)
            pv = jnp.dot(pr.astype(jnp.bfloat16), vv, preferred_element_type=jnp.float32)
            l = pv[:, LANES:]
            o = pv[:, :LANES] / l
            lse = m + jnp.log2(l)
            dst_o[pl.ds(row0, W), :] = jnp.where(first, o[:W], o[W:])
            dst_l[pl.ds(row0, W), :] = jnp.where(first, lse[:W], lse[W:])

        def group(g, carry, nbk=nbk):
            for i in range(DIL_UNROLL):
                u = g * DIL_UNROLL + i
                if nbk == 1:
                    no_prev = 1
                elif i > 0:
                    no_prev = 0
                elif nbk == DIL_UNROLL:
                    no_prev = 1
                else:
                    no_prev = (lax.rem(u, nbk) == 0).astype(jnp.int32)
                one_block(u, no_prev)
            return carry

        assert nbk == 1 or nbk % DIL_UNROLL == 0
        lax.fori_loop(0, n_blocks // DIL_UNROLL, group, 0)

        if d > 1:
            for r in range(d):
                ro_ref[pi, pl.ds(r, L, stride=d), :] = po_ref[r * L:(r + 1) * L, :]
                rl_ref[pi, pl.ds(r, L, stride=d), :] = pl_ref[r * L:(r + 1) * L, :]

    def mix(c, carry):
        row0 = pl.multiple_of(c * W, W)
        ls = [rl_ref[i, pl.ds(row0, W), :] for i in range(3)]
        mx = jnp.maximum(jnp.maximum(ls[0], ls[1]), ls[2])
        es = [jnp.exp2(x - mx) for x in ls]
        den = es[0] + es[1] + es[2]
        num = (es[0] * ro_ref[0, pl.ds(row0, W), :] + es[1] * ro_ref[1, pl.ds(row0, W), :]
               + es[2] * ro_ref[2, pl.ds(row0, W), :])
        o_ref[pl.ds(row0, W), :] = (num / den).astype(o_ref.dtype)
        return carry

    lax.fori_loop(0, n_blocks, mix, 0)


def _dil_attn(proj, slopes, B, S):
    T = proj.shape[0]
    n_pairs = N_DIL_HEADS // 2
    base = 3 * N_DIFF_HEADS * HEAD_DIM // LANES
    W = DIL_W
    return pl.pallas_call(
        functools.partial(_dil_attn_kernel, S=S),
        grid=(B, n_pairs),
        in_specs=[
            pl.BlockSpec(memory_space=pltpu.SMEM),
            pl.BlockSpec((S, LANES), lambda b, p: (b, base + p)),
            pl.BlockSpec((S, LANES), lambda b, p: (b, base + n_pairs + p)),
            pl.BlockSpec((S, LANES), lambda b, p: (b, base + 2 * n_pairs + p)),
        ],
        out_specs=pl.BlockSpec((S, LANES), lambda b, p: (b, p)),
        out_shape=jax.ShapeDtypeStruct((T, N_DIL_HEADS * HEAD_DIM), jnp.bfloat16),
        scratch_shapes=[
            pltpu.VMEM((S, LANES), jnp.float32),
            pltpu.VMEM((S, LANES), jnp.float32),
            pltpu.VMEM((S, LANES), jnp.float32),
            pltpu.VMEM((S, LANES), jnp.bfloat16),
            pltpu.VMEM((S + W, LANES), jnp.bfloat16),
            pltpu.VMEM((S + W, 2 * LANES), jnp.bfloat16),
            pltpu.VMEM((2, 2 * W, 2 * W), jnp.float32),
            pltpu.VMEM((S, LANES), jnp.float32),
            pltpu.VMEM((S, LANES), jnp.float32),
            pltpu.VMEM((3, S, LANES), jnp.float32),
            pltpu.VMEM((3, S, LANES), jnp.float32),
        ],
        compiler_params=_cparams(("arbitrary", "arbitrary")),
        name="dil_attn",
    )(slopes, proj, proj, proj)


def _post_attn_kernel(x_ref, md_ref, ml_ref, wo_ref, nw_ref, rw_ref, rb_ref,
                      x1_ref, h2_ref, route_ref, gate_ref, cnt_ref, run_ref, *, tm):
    i = pl.program_id(0)

    @pl.when(i == 0)
    def _():
        run_ref[...] = jnp.zeros(run_ref.shape, jnp.float32)

    half = md_ref.shape[1]
    y = jnp.dot(md_ref[...], wo_ref[0:half, :], preferred_element_type=jnp.float32)
    y = y + jnp.dot(ml_ref[...], wo_ref[half:, :], preferred_element_type=jnp.float32)
    x1 = x_ref[...] + y
    x1_ref[...] = x1
    ms = jnp.mean(x1 * x1, axis=-1, keepdims=True)
    h2 = (x1 * lax.rsqrt(ms + NORM_EPS)) * nw_ref[...]
    h2_ref[...] = h2
    hh = h2.astype(jnp.bfloat16)
    hl = (h2 - hh.astype(jnp.float32)).astype(jnp.bfloat16)
    lh = jnp.dot(hh, rw_ref[...], preferred_element_type=jnp.float32)
    ll = jnp.dot(hl, rw_ref[...], preferred_element_type=jnp.float32)
    logits = (lh[:, :LANES] + lh[:, LANES:]) + (ll[:, :LANES] + ll[:, LANES:]) + rb_ref[...]
    lane = lax.broadcasted_iota(jnp.int32, (tm, LANES), 1)
    work = jnp.where(lane < N_EXPERTS, logits, NEG_INF)
    vals, idxs, sels = [], [], []
    for _ in range(TOP_K):
        mk = jnp.max(work, axis=1, keepdims=True)
        ik = jnp.min(jnp.where(work == mk, lane, LANES), axis=1, keepdims=True)
        sel = lane == ik
        work = jnp.where(sel, NEG_INF, work)
        vals.append(mk)
        idxs.append(ik)
        sels.append(sel)
    es = [jnp.exp(v - vals[0]) for v in vals]
    den = es[0] + es[1] + es[2] + es[3]
    member = jnp.zeros((tm, LANES), jnp.float32)
    for sel in sels:
        member = jnp.where(sel, 1.0, member)
    r_i = lax.broadcasted_iota(jnp.int32, (tm, tm), 0)
    c_i = lax.broadcasted_iota(jnp.int32, (tm, tm), 1)
    tri = jnp.where(c_i < r_i, 1.0, 0.0).astype(jnp.bfloat16)
    before = jnp.dot(tri, member.astype(jnp.bfloat16), preferred_element_type=jnp.float32)
    before = before + run_ref[...]
    route = jnp.zeros((tm, LANES), jnp.int32)
    gates = jnp.zeros((tm, LANES), jnp.float32)
    for k in range(TOP_K):
        rank = jnp.sum(jnp.where(sels[k], before, 0.0), axis=1, keepdims=True)
        route = jnp.where(lane == k, idxs[k], route)
        route = jnp.where(lane == TOP_K + k, rank.astype(jnp.int32), route)
        gates = jnp.where(lane == k, es[k] / den, gates)
    route_ref[...] = route
    gate_ref[...] = gates
    run_new = run_ref[...] + jnp.sum(member, axis=0, keepdims=True)
    run_ref[...] = run_new
    cnt_ref[...] = jnp.broadcast_to(run_new, cnt_ref.shape).astype(jnp.int32)


def _post_attn(x2d, mix_d, mix_l, wo_bf16, nw, rw_cat, rb, tm=512):
    T, D = x2d.shape
    half = mix_d.shape[1]
    return pl.pallas_call(
        functools.partial(_post_attn_kernel, tm=tm),
        grid=(T // tm,),
        in_specs=[
            pl.BlockSpec((tm, D), lambda i: (i, 0)),
            pl.BlockSpec((tm, half), lambda i: (i, 0)),
            pl.BlockSpec((tm, half), lambda i: (i, 0)),
            pl.BlockSpec((D, D), lambda i: (0, 0)),
            pl.BlockSpec((1, D), lambda i: (0, 0)),
            pl.BlockSpec((D, 2 * LANES), lambda i: (0, 0)),
            pl.BlockSpec((1, LANES), lambda i: (0, 0)),
        ],
        out_specs=[
            pl.BlockSpec((tm, D), lambda i: (i, 0)),
            pl.BlockSpec((tm, D), lambda i: (i, 0)),
            pl.BlockSpec((tm, LANES), lambda i: (i, 0)),
            pl.BlockSpec((tm, LANES), lambda i: (i, 0)),
            pl.BlockSpec((8, LANES), lambda i: (0, 0)),
        ],
        out_shape=[
            jax.ShapeDtypeStruct((T, D), jnp.float32),
            jax.ShapeDtypeStruct((T, D), jnp.float32),
            jax.ShapeDtypeStruct((T, LANES), jnp.int32),
            jax.ShapeDtypeStruct((T, LANES), jnp.float32),
            jax.ShapeDtypeStruct((8, LANES), jnp.int32),
        ],
        scratch_shapes=[pltpu.VMEM((1, LANES), jnp.float32)],
        compiler_params=_cparams(("arbitrary",)),
        name="post_attn",
    )(x2d, mix_d, mix_l, wo_bf16, nw, rw_cat, rb)


def _dispatch_kernel(dest_ref, cnt_ref, pcnt_ref, pst_ref, h_ref, xs_ref, z_ref, sem, zsem, *, td):
    i = pl.program_id(0)
    n = pl.num_programs(0)

    def row_copy(t, d):
        return pltpu.make_async_copy(h_ref.at[pl.ds(t, 1)], xs_ref.at[pl.ds(d, 1)], sem)

    def issue(t, carry):
        for k in range(TOP_K):
            row_copy(t, dest_ref[(i * td + t) * TOP_K + k]).start(priority=k % 2)
        return carry

    lax.fori_loop(0, td, issue, 0)

    for k in range(TOP_K):
        pltpu.make_async_copy(h_ref, xs_ref.at[pl.ds(0, td)], sem).wait()

    @pl.when(i == n - 1)
    def _():
        z_ref[...] = jnp.zeros(z_ref.shape, z_ref.dtype)

        def pad_copy(row):
            return pltpu.make_async_copy(z_ref.at[pl.ds(0, 1)], xs_ref.at[pl.ds(row, 1)], zsem)

        def per_expert(e, carry):
            base = pst_ref[e]

            def start(r, c):
                pad_copy(base + r).start()
                return c

            def wait(r, c):
                pad_copy(base + r).wait()
                return c

            lax.fori_loop(cnt_ref[e], pcnt_ref[e], start, 0)
            lax.fori_loop(cnt_ref[e], pcnt_ref[e], wait, 0)
            return carry

        lax.fori_loop(0, N_EXPERTS, per_expert, 0)

        def tail_copy(c):
            row = pl.multiple_of(c * td, td)
            return pltpu.make_async_copy(z_ref, xs_ref.at[pl.ds(row, td)], zsem)

        end = pst_ref[N_EXPERTS - 1] + pcnt_ref[N_EXPERTS - 1]
        first_chunk = end // td
        n_chunks = xs_ref.shape[0] // td

        def tail_start(c, carry):
            tail_copy(c).start()
            return carry

        def tail_wait(c, carry):
            tail_copy(c).wait()
            return carry

        lax.fori_loop(first_chunk, n_chunks, tail_start, 0)
        lax.fori_loop(first_chunk, n_chunks, tail_wait, 0)


def _dispatch(dest_flat, counts, pcounts, pstarts, h2, R, td=256):
    T, D = h2.shape
    return pl.pallas_call(
        functools.partial(_dispatch_kernel, td=td),
        grid_spec=pltpu.PrefetchScalarGridSpec(
            num_scalar_prefetch=4,
            grid=(T // td,),
            in_specs=[pl.BlockSpec((td, D), lambda i, *_: (i, 0))],
            out_specs=pl.BlockSpec(memory_space=pl.ANY),
            scratch_shapes=[
                pltpu.VMEM((td, D), h2.dtype),
                pltpu.SemaphoreType.DMA,
                pltpu.SemaphoreType.DMA,
            ],
        ),
        out_shape=jax.ShapeDtypeStruct((R, D), h2.dtype),
        compiler_params=_cparams(("arbitrary",)),
        name="dispatch",
    )(dest_flat, counts, pcounts, pstarts, h2)


def _experts_kernel(be_ref, used_ref, xs_ref, wgu_ref, bgu_ref, wd_ref, bd_ref, o_ref,
                    wgu_bf, wd_bf, *, n_chunk):
    i = pl.program_id(0)
    prev = be_ref[jnp.maximum(i - 1, 0)]
    new_expert = jnp.logical_or(i == 0, be_ref[i] != prev)

    @pl.when(new_expert)
    def _():
        wgu_bf[...] = wgu_ref[0].astype(jnp.bfloat16)
        wd_bf[...] = wd_ref[0].astype(jnp.bfloat16)

    @pl.when(i < used_ref[0])
    def _():
        xb = xs_ref[...].astype(jnp.bfloat16)
        acc = jnp.zeros(o_ref.shape, jnp.float32)
        for c in range(D_FF // n_chunk):
            gs = slice(c * n_chunk, (c + 1) * n_chunk)
            us = slice(D_FF + c * n_chunk, D_FF + (c + 1) * n_chunk)
            g = jnp.dot(xb, wgu_bf[:, gs], preferred_element_type=jnp.float32) + bgu_ref[0, :, gs]
            u = jnp.dot(xb, wgu_bf[:, us], preferred_element_type=jnp.float32) + bgu_ref[0, :, us]
            g = jnp.minimum(g, SWIGLU_LIMIT)
            u = jnp.clip(u, -SWIGLU_LIMIT, SWIGLU_LIMIT)
            act = (u + 1.0) * (g * jax.nn.sigmoid(SWIGLU_ALPHA * g))
            acc = acc + jnp.dot(act.astype(jnp.bfloat16), wd_bf[gs, :],
                                preferred_element_type=jnp.float32)
        o_ref[...] = acc + bd_ref[0]

    @pl.when(i >= used_ref[0])
    def _():
        o_ref[...] = jnp.zeros(o_ref.shape, o_ref.dtype)


def _experts(block_e, used, xs, w_gate_up, b_gate_up, w_down, b_down, blk=MOE_BLK):
    R, D = xs.shape
    n_blocks = R // blk
    F2 = w_gate_up.shape[2]

    def row_map(i, be, used):
        return (i, 0)

    return pl.pallas_call(
        functools.partial(_experts_kernel, n_chunk=512),
        grid_spec=pltpu.PrefetchScalarGridSpec(
            num_scalar_prefetch=2,
            grid=(n_blocks,),
            in_specs=[
                pl.BlockSpec((blk, D), row_map),
                pl.BlockSpec((1, D, F2), lambda i, be, used: (be[i], 0, 0)),
                pl.BlockSpec((1, 1, F2), lambda i, be, used: (be[i], 0, 0)),
                pl.BlockSpec((1, D_FF, D), lambda i, be, used: (be[i], 0, 0)),
                pl.BlockSpec((1, 1, D), lambda i, be, used: (be[i], 0, 0)),
            ],
            out_specs=pl.BlockSpec((blk, D), row_map),
            scratch_shapes=[
                pltpu.VMEM((D, F2), jnp.bfloat16),
                pltpu.VMEM((D_FF, D), jnp.bfloat16),
            ],
        ),
        out_shape=jax.ShapeDtypeStruct((R, D), jnp.float32),
        compiler_params=_cparams(("arbitrary",)),
        name="experts",
    )(block_e, used, xs, w_gate_up, b_gate_up.reshape(N_EXPERTS, 1, F2),
      w_down, b_down.reshape(N_EXPERTS, 1, D))


def _combine_kernel(dest_ref, x1_ref, g_ref, nw_ref, rows_ref, o_ref, buf_ref, sems, *, tc):
    i = pl.program_id(0)
    n = pl.num_programs(0)
    slot = i & 1

    def issue_tile(tile, s):
        def issue(t, carry):
            for k in range(TOP_K):
                d = dest_ref[(tile * tc + t) * TOP_K + k]
                pltpu.make_async_copy(rows_ref.at[pl.ds(d, 1)], buf_ref.at[s, k, pl.ds(t, 1)],
                                      sems.at[s]).start(priority=k % 2)
            return carry

        lax.fori_loop(0, tc, issue, 0)

    @pl.when(i == 0)
    def _():
        issue_tile(0, 0)

    @pl.when(i + 1 < n)
    def _():
        issue_tile(i + 1, 1 - slot)

    for k in range(TOP_K):
        pltpu.make_async_copy(rows_ref.at[pl.ds(0, tc)], buf_ref.at[slot, k], sems.at[slot]).wait()

    g = g_ref[...]
    y = x1_ref[...]
    for k in range(TOP_K):
        y = y + g[:, k:k + 1] * buf_ref[slot, k]
    ms = jnp.mean(y * y, axis=-1, keepdims=True)
    o_ref[...] = (y * lax.rsqrt(ms + NORM_EPS)) * nw_ref[...]


def _combine(dest_flat, x1, gates, nw, rows, tc=128):
    T, D = x1.shape
    return pl.pallas_call(
        functools.partial(_combine_kernel, tc=tc),
        grid_spec=pltpu.PrefetchScalarGridSpec(
            num_scalar_prefetch=1,
            grid=(T // tc,),
            in_specs=[
                pl.BlockSpec((tc, D), lambda i, *_: (i, 0)),
                pl.BlockSpec((tc, LANES), lambda i, *_: (i, 0)),
                pl.BlockSpec((1, D), lambda i, *_: (0, 0)),
                pl.BlockSpec(memory_space=pl.ANY),
            ],
            out_specs=pl.BlockSpec((tc, D), lambda i, *_: (i, 0)),
            scratch_shapes=[
                pltpu.VMEM((2, TOP_K, tc, D), jnp.float32),
                pltpu.SemaphoreType.DMA((2,)),
            ],
        ),
        out_shape=jax.ShapeDtypeStruct((T, D), jnp.float32),
        compiler_params=_cparams(("arbitrary",)),
        name="combine",
    )(dest_flat, x1, gates, nw, rows)


def kernel(x, attn_norm_w, w_in, diff_lambda_q1, diff_lambda_k1, diff_lambda_q2, diff_lambda_k2,
           diff_subln_w, w_out, ffn_norm_w, router_w, router_b, w_gate_up, b_gate_up, w_down,
           b_down, final_norm_w):
    B, S, D = x.shape
    T = B * S
    f32 = jnp.float32
    n = jnp.arange(1, N_HEADS_TOTAL + 1, dtype=f32)
    slopes = jnp.exp2(-8.0 * n / N_HEADS_TOTAL)
    diff_slopes = slopes[0::2] * LOG2E
    dil_slopes = slopes[1::2] * LOG2E
    depth = attn_norm_w.shape[0]
    assert depth == 1, "the combine kernel fuses the final norm, so exactly one layer is supported"
    x2d = x.reshape(T, D)
    diff_w = N_DIFF_HEADS * HEAD_DIM
    colscale = jnp.concatenate([
        jnp.full((diff_w,), DIFF_QK_DIM ** -0.5 * LOG2E, f32), jnp.ones((2 * diff_w,), f32),
        jnp.full((N_DIL_HEADS * HEAD_DIM,), HEAD_DIM ** -0.5 * LOG2E, f32),
        jnp.ones((2 * N_DIL_HEADS * HEAD_DIM,), f32)]).reshape(1, IN_COLS)
    n_blocks = -(-(T * TOP_K + N_EXPERTS * (MOE_BLK - 1)) // MOE_BLK)
    R = n_blocks * MOE_BLK

    for l in range(depth):
        proj = _in_proj(x2d, attn_norm_w[l].reshape(1, D), w_in[l].astype(jnp.bfloat16), colscale)
        lam_init = 0.8 - 0.6 * math.exp(-0.3 * l)
        lam = (jnp.exp(jnp.sum(diff_lambda_q1[l] * diff_lambda_k1[l]).astype(f32))
               - jnp.exp(jnp.sum(diff_lambda_q2[l] * diff_lambda_k2[l]).astype(f32))
               + lam_init).reshape(1)
        subln2 = jnp.tile(diff_subln_w[l], 2).reshape(1, LANES)
        mix_d = _diff_attn(proj, diff_slopes, lam, subln2, B, S, 1.0 - lam_init)
        mix_l = _dil_attn(proj, dil_slopes, B, S)

        rw = jnp.zeros((D, LANES), f32).at[:, :N_EXPERTS].set(router_w[l])
        rw_hi = rw.astype(jnp.bfloat16)
        rw_lo = (rw - rw_hi.astype(f32)).astype(jnp.bfloat16)
        rb = jnp.zeros((1, LANES), f32).at[0, :N_EXPERTS].set(router_b[l])
        x1, h2, route, gates, cnt = _post_attn(
            x2d, mix_d, mix_l, w_out[l].astype(jnp.bfloat16), ffn_norm_w[l].reshape(1, D),
            jnp.concatenate([rw_hi, rw_lo], axis=1), rb)

        counts = cnt[0, :N_EXPERTS]
        pcounts = (counts + MOE_BLK - 1) // MOE_BLK * MOE_BLK
        pends = jnp.cumsum(pcounts)
        pstarts = pends - pcounts
        top_i = route[:, :TOP_K]
        rank = route[:, TOP_K:2 * TOP_K]
        onehot = top_i[..., None] == jnp.arange(N_EXPERTS, dtype=jnp.int32)
        dest = jnp.sum(jnp.where(onehot, pstarts.astype(jnp.int32), 0), axis=-1) + rank
        dest_flat = dest.reshape(T * TOP_K).astype(jnp.int32)
        used = (pends[-1] // MOE_BLK).astype(jnp.int32)
        blk_start = jnp.arange(n_blocks, dtype=jnp.int32) * MOE_BLK
        be = jnp.minimum(jnp.sum(pends[None, :] <= blk_start[:, None], axis=1), N_EXPERTS - 1)
        be_last = jnp.max(jnp.where(jnp.arange(n_blocks) < used, be, 0))
        block_e = jnp.where(jnp.arange(n_blocks) < used, be, be_last).astype(jnp.int32)

        xs = _dispatch(dest_flat, counts.astype(jnp.int32), pcounts.astype(jnp.int32),
                       pstarts.astype(jnp.int32), h2, R)
        rows = _experts(block_e, used.reshape(1), xs, w_gate_up[l], b_gate_up[l], w_down[l], b_down[l])
        x2d = _combine(dest_flat, x1, gates, final_norm_w.reshape(1, D), rows)
    return x2d.reshape(B, S, D)
```

```python
import functools
import math

import jax
import jax.numpy as jnp
from jax import lax
from jax.experimental import pallas as pl
from jax.experimental.pallas import tpu as pltpu

D_MODEL = 1024
HEAD_DIM = 64
N_DIFF_HEADS = 8
DIFF_QK_DIM = 32
N_DIL_HEADS = 8
N_HEADS_TOTAL = 16
DIL_PATTERNS = ((128, 1), (512, 4), (2048, 16))
DIL_W = 128
N_EXPERTS = 32
TOP_K = 4
D_FF = D_MODEL
SWIGLU_LIMIT = 7.0
SWIGLU_ALPHA = 1.702
NORM_EPS = 1e-5
IN_COLS = 3072
LANES = 128
SUBLANES = 8
DIL_UNROLL = 8
LOG2E = math.log2(math.e)
MOE_BLK = 512
VMEM_LIMIT = 56 * 1024 * 1024

NEG_INF = float("-inf")


def _cparams(sem, vmem=VMEM_LIMIT):
    return pltpu.CompilerParams(dimension_semantics=sem, vmem_limit_bytes=vmem)


def _in_proj_kernel(x_ref, nw_ref, w_ref, cs_ref, o_ref, *, n_chunk):
    xf = x_ref[...]
    ms = jnp.mean(xf * xf, axis=-1, keepdims=True)
    h = (xf * lax.rsqrt(ms + NORM_EPS)) * nw_ref[...]
    hb = h.astype(jnp.bfloat16)
    n_cols = o_ref.shape[1]
    for c in range(n_cols // n_chunk):
        sl = slice(c * n_chunk, (c + 1) * n_chunk)
        r = jnp.dot(hb, w_ref[:, sl], preferred_element_type=jnp.float32)
        o_ref[:, sl] = (r * cs_ref[:, sl]).astype(o_ref.dtype)


def _in_proj(x2d, nw, w_bf16, colscale, tm=512):
    T, D = x2d.shape
    N = w_bf16.shape[1]
    return pl.pallas_call(
        functools.partial(_in_proj_kernel, n_chunk=512),
        grid=(T // tm,),
        in_specs=[
            pl.BlockSpec((tm, D), lambda i: (i, 0)),
            pl.BlockSpec((1, D), lambda i: (0, 0)),
            pl.BlockSpec((D, N), lambda i: (0, 0)),
            pl.BlockSpec((1, N), lambda i: (0, 0)),
        ],
        out_specs=pl.BlockSpec((tm, N), lambda i: (i, 0)),
        out_shape=jax.ShapeDtypeStruct((T, N), jnp.bfloat16),
        compiler_params=_cparams(("arbitrary",)),
        name="in_proj",
    )(x2d, nw, w_bf16, colscale)


def _diff_attn_kernel(slopes_ref, lam_ref, q_ref, k_ref, v_ref, w_ref, o_ref,
                      q4t_ref, pos_ref, vt_ref, mask_ref, s_ref, mt_ref, m_ref, acc_ref,
                      *, tq, tk, post_scale):
    assert tq == tk
    p = pl.program_id(1)
    qi = pl.program_id(2)
    n_lanes = 4 * tq
    n_kv = v_ref.shape[0] // tk
    vt_rows = vt_ref.shape[2]

    lane = lax.broadcasted_iota(jnp.int32, (1, n_lanes), 1)
    slope_l = jnp.where(lane < 2 * tq, slopes_ref[2 * p], slopes_ref[2 * p + 1])

    @pl.when(qi == 0)
    def _():
        ones = jnp.ones((vt_rows - HEAD_DIM, tk), jnp.bfloat16)
        for jj in range(n_kv):
            vt = v_ref[jj * tk:(jj + 1) * tk, :].astype(jnp.float32).T.astype(jnp.bfloat16)
            for h in range(2):
                vt_ref[jj, h, 0:HEAD_DIM, :] = vt[h * HEAD_DIM:(h + 1) * HEAD_DIM, :]
                vt_ref[jj, h, HEAD_DIM:vt_rows, :] = ones
        hi = slope_l.astype(jnp.bfloat16).astype(jnp.float32)
        mid = (slope_l - hi).astype(jnp.bfloat16).astype(jnp.float32)
        lo = slope_l - hi - mid
        r16 = lax.broadcasted_iota(jnp.int32, (16, n_lanes), 0)
        rows = jnp.where(r16 == 0, hi, jnp.where(r16 == 1, mid, jnp.where(r16 == 2, lo, 0.0)))
        q4t_ref[LANES:LANES + 16, :] = rows.astype(jnp.bfloat16)
        q4t_ref[LANES + 16:, :] = jnp.zeros((LANES - 16, n_lanes), jnp.bfloat16)
        key_c = lax.broadcasted_iota(jnp.int32, (tk, LANES), 0)
        col_c = lax.broadcasted_iota(jnp.int32, (tk, LANES), 1)
        pos_ref[...] = jnp.where(col_c < 3, key_c, 0).astype(jnp.float32).astype(jnp.bfloat16)
        key_i = lax.broadcasted_iota(jnp.int32, (tk, n_lanes), 0)
        qry_i = lax.broadcasted_iota(jnp.int32, (tk, n_lanes), 1) & (tq - 1)
        mask_ref[0] = jnp.zeros((tk, n_lanes), jnp.float32)
        mask_ref[1] = jnp.where(key_i <= qry_i, 0.0, NEG_INF)

    qt = q_ref[...].astype(jnp.float32).T
    sub = lax.broadcasted_iota(jnp.int32, (LANES, tq), 0) >> 5
    for c in range(4):
        q4t_ref[0:LANES, c * tq:(c + 1) * tq] = jnp.where(sub == c, qt, 0.0).astype(jnp.bfloat16)
    acc_ref[...] = jnp.zeros(acc_ref.shape, jnp.float32)

    def tile_off(n):
        return slope_l * jnp.asarray(n * tk).astype(jnp.float32)

    def scores(n, slot, mask):
        start = pl.multiple_of(n * tk, tk)
        ka = jnp.concatenate([k_ref[pl.ds(start, tk), :], pos_ref[...]], axis=1)
        s = jnp.dot(ka, q4t_ref[...], preferred_element_type=jnp.float32)
        if mask == "diag":
            s = s + mask_ref[1]
        elif mask == "maybe":
            s = s + mask_ref[(n == qi).astype(jnp.int32)]
        s_ref[slot] = s
        mt_ref[slot] = jnp.max(s, axis=0, keepdims=True) + tile_off(n)

    def weights(n, slot, m_prev):
        m_new = jnp.maximum(m_prev, mt_ref[slot])
        alpha = jnp.exp2(m_prev - m_new)
        pb = jnp.exp2(s_ref[slot] - (m_new - tile_off(n))).astype(jnp.bfloat16)
        for h in range(2):
            cols = slice(h * 2 * tq, (h + 1) * 2 * tq)
            pv = jnp.dot(vt_ref[n, h], pb[:, cols], preferred_element_type=jnp.float32)
            acc_ref[h] = alpha[:, cols] * acc_ref[h] + pv
        return m_new

    scores(0, 0, "maybe")

    def pair(u, m):
        scores(2 * u + 1, 1, None)
        m = weights(2 * u, 0, m)
        scores(2 * u + 2, 0, "maybe")
        return weights(2 * u + 1, 1, m)

    m_ref[...] = lax.fori_loop(0, qi // 2, pair, jnp.full((1, n_lanes), NEG_INF, jnp.float32))

    @pl.when((qi & 1) == 1)
    def _():
        scores(qi, 1, "diag")
        weights(qi, 1, weights(qi - 1, 0, m_ref[...]))

    @pl.when((qi & 1) == 0)
    def _():
        weights(qi, 0, m_ref[...])

    lam = lam_ref[0]
    ys = []
    for h in range(2):
        acc = acc_ref[h]
        num = acc[0:HEAD_DIM]
        l = acc[HEAD_DIM:HEAD_DIM + 1]
        o = num[:, :tq] / l[:, :tq] - lam * (num[:, tq:] / l[:, tq:])
        ms = jnp.mean(o * o, axis=0, keepdims=True)
        ys.append(o * lax.rsqrt(ms + NORM_EPS))
    y = jnp.concatenate(ys, axis=0).T
    o_ref[...] = ((y * w_ref[...]) * post_scale).astype(o_ref.dtype)


def _diff_attn(proj, slopes, lam, subln_w2, B, S, post_scale, tq=256, tk=256):
    T = proj.shape[0]
    nq = S // tq
    n_pairs = N_DIFF_HEADS // 2
    kern = functools.partial(_diff_attn_kernel, tq=tq, tk=tk, post_scale=post_scale)
    return pl.pallas_call(
        kern,
        grid=(B, n_pairs, nq),
        in_specs=[
            pl.BlockSpec(memory_space=pltpu.SMEM),
            pl.BlockSpec(memory_space=pltpu.SMEM),
            pl.BlockSpec((tq, LANES), lambda b, p, i: (b * nq + i, p)),
            pl.BlockSpec((S, LANES), lambda b, p, i: (b, n_pairs + p)),
            pl.BlockSpec((S, LANES), lambda b, p, i: (b, 2 * n_pairs + p)),
            pl.BlockSpec((1, LANES), lambda b, p, i: (0, 0)),
        ],
        out_specs=pl.BlockSpec((tq, LANES), lambda b, p, i: (b * nq + i, p)),
        out_shape=jax.ShapeDtypeStruct((T, N_DIFF_HEADS * HEAD_DIM), jnp.bfloat16),
        scratch_shapes=[
            pltpu.VMEM((2 * LANES, 4 * tq), jnp.bfloat16),
            pltpu.VMEM((tk, LANES), jnp.bfloat16),
            pltpu.VMEM((S // tk, 2, HEAD_DIM + 16, tk), jnp.bfloat16),
            pltpu.VMEM((2, tk, 4 * tq), jnp.float32),
            pltpu.VMEM((2, tk, 4 * tq), jnp.float32),
            pltpu.VMEM((2, 1, 4 * tq), jnp.float32),
            pltpu.VMEM((1, 4 * tq), jnp.float32),
            pltpu.VMEM((2, HEAD_DIM + 16, 2 * tq), jnp.float32),
        ],
        compiler_params=_cparams(("arbitrary", "arbitrary", "arbitrary")),
        name="diff_attn",
    )(slopes, lam, proj, proj, proj, subln_w2)


def _dil_attn_kernel(slopes_ref, q_ref, k_ref, v_ref, o_ref,
                     qf_ref, kf_ref, vf_ref, qd_ref, kd_ref, vd_ref, a_ref,
                     po_ref, pl_ref, ro_ref, rl_ref, *, S):
    p = pl.program_id(1)
    W = DIL_W
    n_blocks = S // W
    qf_ref[...] = q_ref[...].astype(jnp.float32)
    kf_ref[...] = k_ref[...].astype(jnp.float32)
    vf_ref[...] = v_ref[...].astype(jnp.float32)
    @pl.when((pl.program_id(0) == 0) & (p == 0))
    def _():
        kd_ref[0:W, :] = jnp.zeros((W, LANES), jnp.bfloat16)
        vd_ref[0:W, 0:LANES] = jnp.zeros((W, LANES), jnp.bfloat16)
        vd_ref[:, LANES:] = jnp.ones((S + W, LANES), jnp.bfloat16)

    lane = lax.broadcasted_iota(jnp.int32, (W, LANES), 1)
    first = lane < HEAD_DIM
    r_i = lax.broadcasted_iota(jnp.int32, (W, 2 * W), 0)
    c_i = lax.broadcasted_iota(jnp.int32, (W, 2 * W), 1)
    dist = W + r_i - c_i
    valid = (dist >= 0) & (dist <= W)
    dist_f = dist.astype(jnp.float32)

    for pi, (_, d) in enumerate(DIL_PATTERNS):
        L = S // d
        nbk = L // W
        if d == 1:
            qd_ref[...] = q_ref[...]
            kd_ref[W:W + S, :] = k_ref[...]
            vd_ref[W:W + S, 0:LANES] = v_ref[...]
        else:
            for r in range(d):
                qd_ref[r * L:(r + 1) * L, :] = qf_ref[pl.ds(r, L, stride=d), :].astype(jnp.bfloat16)
                kd_ref[W + r * L:W + (r + 1) * L, :] = kf_ref[pl.ds(r, L, stride=d), :].astype(jnp.bfloat16)
                vd_ref[W + r * L:W + (r + 1) * L, 0:LANES] = vf_ref[pl.ds(r, L, stride=d), :].astype(jnp.bfloat16)
        for h in range(2):
            slope = slopes_ref[2 * p + h] * float(d)
            full = jnp.where(valid, slope * dist_f, jnp.inf)
            a_ref[0, h * W:(h + 1) * W, :] = full
            a_ref[1, h * W:(h + 1) * W, :] = jnp.where(c_i < W, jnp.inf, full)

        dst_o = ro_ref.at[pi] if d == 1 else po_ref
        dst_l = rl_ref.at[pi] if d == 1 else pl_ref

        def one_block(u, no_prev, dst_o=dst_o, dst_l=dst_l):
            row0 = pl.multiple_of(u * W, W)
            qb = qd_ref[pl.ds(row0, W), :].astype(jnp.float32)
            kk = kd_ref[pl.ds(row0, 2 * W), :]
            vv = vd_ref[pl.ds(row0, 2 * W), :]
            q2 = jnp.concatenate([jnp.where(first, qb, 0.0), jnp.where(first, 0.0, qb)],
                                 axis=0).astype(jnp.bfloat16)
            s = lax.dot_general(q2, kk, (((1,), (1,)), ((), ())),
                                preferred_element_type=jnp.float32)
            s = s - a_ref[no_prev]
            m = jnp.max(s, axis=1, keepdims=True)
            pr = jnp.exp2(s - m)
            pv = jnp.dot(pr.astype(jnp.bfloat16), vv, preferred_element_type=jnp.float32)
            l = pv[:, LANES:]
            o = pv[:, :LANES] / l
            lse = m + jnp.log2(l)
            dst_o[pl.ds(row0, W), :] = jnp.where(first, o[:W], o[W:])
            dst_l[pl.ds(row0, W), :] = jnp.where(first, lse[:W], lse[W:])

        def group(g, carry, nbk=nbk):
            for i in range(DIL_UNROLL):
                u = g * DIL_UNROLL + i
                if DIL_UNROLL % nbk == 0:
                    no_prev = int(i % nbk == 0)
                elif i > 0:
                    no_prev = 0
                else:
                    no_prev = (lax.rem(u, nbk) == 0).astype(jnp.int32)
                one_block(u, no_prev)
            return carry

        assert DIL_UNROLL % nbk == 0 or nbk % DIL_UNROLL == 0
        lax.fori_loop(0, n_blocks // DIL_UNROLL, group, 0)

        if d > 1:
            for r in range(d):
                ro_ref[pi, pl.ds(r, L, stride=d), :] = po_ref[r * L:(r + 1) * L, :]
                rl_ref[pi, pl.ds(r, L, stride=d), :] = pl_ref[r * L:(r + 1) * L, :]

    def mix(c, carry):
        row0 = pl.multiple_of(c * W, W)
        ls = [rl_ref[i, pl.ds(row0, W), :] for i in range(3)]
        mx = jnp.maximum(jnp.maximum(ls[0], ls[1]), ls[2])
        es = [jnp.exp2(x - mx) for x in ls]
        den = es[0] + es[1] + es[2]
        num = (es[0] * ro_ref[0, pl.ds(row0, W), :] + es[1] * ro_ref[1, pl.ds(row0, W), :]
               + es[2] * ro_ref[2, pl.ds(row0, W), :])
        o_ref[pl.ds(row0, W), :] = (num / den).astype(o_ref.dtype)
        return carry

    lax.fori_loop(0, n_blocks, mix, 0)


def _dil_attn(proj, slopes, B, S):
    T = proj.shape[0]
    n_pairs = N_DIL_HEADS // 2
    base = 3 * N_DIFF_HEADS * HEAD_DIM // LANES
    W = DIL_W
    return pl.pallas_call(
        functools.partial(_dil_attn_kernel, S=S),
        grid=(B, n_pairs),
        in_specs=[
            pl.BlockSpec(memory_space=pltpu.SMEM),
            pl.BlockSpec((S, LANES), lambda b, p: (b, base + p)),
            pl.BlockSpec((S, LANES), lambda b, p: (b, base + n_pairs + p)),
            pl.BlockSpec((S, LANES), lambda b, p: (b, base + 2 * n_pairs + p)),
        ],
        out_specs=pl.BlockSpec((S, LANES), lambda b, p: (b, p)),
        out_shape=jax.ShapeDtypeStruct((T, N_DIL_HEADS * HEAD_DIM), jnp.bfloat16),
        scratch_shapes=[
            pltpu.VMEM((S, LANES), jnp.float32),
            pltpu.VMEM((S, LANES), jnp.float32),
            pltpu.VMEM((S, LANES), jnp.float32),
            pltpu.VMEM((S, LANES), jnp.bfloat16),
            pltpu.VMEM((S + W, LANES), jnp.bfloat16),
            pltpu.VMEM((S + W, 2 * LANES), jnp.bfloat16),
            pltpu.VMEM((2, 2 * W, 2 * W), jnp.float32),
            pltpu.VMEM((S, LANES), jnp.float32),
            pltpu.VMEM((S, LANES), jnp.float32),
            pltpu.VMEM((3, S, LANES), jnp.float32),
            pltpu.VMEM((3, S, LANES), jnp.float32),
        ],
        compiler_params=_cparams(("arbitrary", "arbitrary")),
        name="dil_attn",
    )(slopes, proj, proj, proj)


def _post_attn_kernel(x_ref, md_ref, ml_ref, wo_ref, nw_ref, rw_ref, rb_ref,
                      x1_ref, h2_ref, route_ref, gate_ref, cnt_ref, run_ref, *, tm):
    i = pl.program_id(0)

    @pl.when(i == 0)
    def _():
        run_ref[...] = jnp.zeros(run_ref.shape, jnp.float32)

    half = md_ref.shape[1]
    y = jnp.dot(md_ref[...], wo_ref[0:half, :], preferred_element_type=jnp.float32)
    y = y + jnp.dot(ml_ref[...], wo_ref[half:, :], preferred_element_type=jnp.float32)
    x1 = x_ref[...] + y
    x1_ref[...] = x1
    ms = jnp.mean(x1 * x1, axis=-1, keepdims=True)
    h2 = (x1 * lax.rsqrt(ms + NORM_EPS)) * nw_ref[...]
    for c in range(D_MODEL // LANES):
        h2_ref[pl.ds(c, tm, stride=SUBLANES), :] = h2[:, c * LANES:(c + 1) * LANES]
    hh = h2.astype(jnp.bfloat16)
    hl = (h2 - hh.astype(jnp.float32)).astype(jnp.bfloat16)
    lh = jnp.dot(hh, rw_ref[...], preferred_element_type=jnp.float32)
    ll = jnp.dot(hl, rw_ref[...], preferred_element_type=jnp.float32)
    logits = (lh[:, :LANES] + lh[:, LANES:]) + (ll[:, :LANES] + ll[:, LANES:]) + rb_ref[...]
    lane = lax.broadcasted_iota(jnp.int32, (tm, LANES), 1)
    work = jnp.where(lane < N_EXPERTS, logits, NEG_INF)
    vals, idxs, sels = [], [], []
    for _ in range(TOP_K):
        mk = jnp.max(work, axis=1, keepdims=True)
        ik = jnp.min(jnp.where(work == mk, lane, LANES), axis=1, keepdims=True)
        sel = lane == ik
        work = jnp.where(sel, NEG_INF, work)
        vals.append(mk)
        idxs.append(ik)
        sels.append(sel)
    es = [jnp.exp(v - vals[0]) for v in vals]
    den = es[0] + es[1] + es[2] + es[3]
    member = jnp.zeros((tm, LANES), jnp.float32)
    for sel in sels:
        member = jnp.where(sel, 1.0, member)
    r_i = lax.broadcasted_iota(jnp.int32, (tm, tm), 0)
    c_i = lax.broadcasted_iota(jnp.int32, (tm, tm), 1)
    tri = jnp.where(c_i < r_i, 1.0, 0.0).astype(jnp.bfloat16)
    before = jnp.dot(tri, member.astype(jnp.bfloat16), preferred_element_type=jnp.float32)
    before = before + run_ref[...]
    route = jnp.zeros((tm, LANES), jnp.int32)
    gates = jnp.zeros((tm, LANES), jnp.float32)
    for k in range(TOP_K):
        rank = jnp.sum(jnp.where(sels[k], before, 0.0), axis=1, keepdims=True)
        route = jnp.where(lane == k, idxs[k], route)
        route = jnp.where(lane == TOP_K + k, rank.astype(jnp.int32), route)
        gates = jnp.where(lane == k, es[k] / den, gates)
    route_ref[...] = route
    gate_ref[...] = gates
    run_new = run_ref[...] + jnp.sum(member, axis=0, keepdims=True)
    run_ref[...] = run_new
    cnt_ref[...] = jnp.broadcast_to(run_new, cnt_ref.shape).astype(jnp.int32)


def _post_attn(x2d, mix_d, mix_l, wo_bf16, nw, rw_cat, rb, tm=512):
    T, D = x2d.shape
    half = mix_d.shape[1]
    return pl.pallas_call(
        functools.partial(_post_attn_kernel, tm=tm),
        grid=(T // tm,),
        in_specs=[
            pl.BlockSpec((tm, D), lambda i: (i, 0)),
            pl.BlockSpec((tm, half), lambda i: (i, 0)),
            pl.BlockSpec((tm, half), lambda i: (i, 0)),
            pl.BlockSpec((D, D), lambda i: (0, 0)),
            pl.BlockSpec((1, D), lambda i: (0, 0)),
            pl.BlockSpec((D, 2 * LANES), lambda i: (0, 0)),
            pl.BlockSpec((1, LANES), lambda i: (0, 0)),
        ],
        out_specs=[
            pl.BlockSpec((tm, D), lambda i: (i, 0)),
            pl.BlockSpec((tm * SUBLANES, LANES), lambda i: (i, 0)),
            pl.BlockSpec((tm, LANES), lambda i: (i, 0)),
            pl.BlockSpec((tm, LANES), lambda i: (i, 0)),
            pl.BlockSpec((8, LANES), lambda i: (0, 0)),
        ],
        out_shape=[
            jax.ShapeDtypeStruct((T, D), jnp.float32),
            jax.ShapeDtypeStruct((T * SUBLANES, LANES), jnp.float32),
            jax.ShapeDtypeStruct((T, LANES), jnp.int32),
            jax.ShapeDtypeStruct((T, LANES), jnp.float32),
            jax.ShapeDtypeStruct((8, LANES), jnp.int32),
        ],
        scratch_shapes=[pltpu.VMEM((1, LANES), jnp.float32)],
        compiler_params=_cparams(("arbitrary",)),
        name="post_attn",
    )(x2d, mix_d, mix_l, wo_bf16, nw, rw_cat, rb)


def _token_rows(ref, t, n_tokens=1):
    start = pl.multiple_of(t * SUBLANES, SUBLANES)
    return ref.at[pl.ds(start, n_tokens * SUBLANES), :]


def _dispatch_kernel(dest_ref, cnt_ref, pcnt_ref, pst_ref, h_ref, xs_ref, z_ref, sem, zsem, *, td):
    i = pl.program_id(0)
    n = pl.num_programs(0)

    def issue(t, carry):
        for k in range(TOP_K):
            pltpu.make_async_copy(_token_rows(h_ref, t),
                                  _token_rows(xs_ref, dest_ref[(i * td + t) * TOP_K + k]),
                                  sem).start(priority=k % 2)
        return carry

    lax.fori_loop(0, td, issue, 0)

    for k in range(TOP_K):
        pltpu.make_async_copy(h_ref, _token_rows(xs_ref, 0, td), sem).wait()

    @pl.when(i == n - 1)
    def _():
        z_ref[...] = jnp.zeros(z_ref.shape, z_ref.dtype)

        def pad_copy(row):
            return pltpu.make_async_copy(_token_rows(z_ref, 0), _token_rows(xs_ref, row), zsem)

        def per_expert(e, carry):
            base = pst_ref[e]

            def start(r, c):
                pad_copy(base + r).start()
                return c

            def wait(r, c):
                pad_copy(base + r).wait()
                return c

            lax.fori_loop(cnt_ref[e], pcnt_ref[e], start, 0)
            lax.fori_loop(cnt_ref[e], pcnt_ref[e], wait, 0)
            return carry

        lax.fori_loop(0, N_EXPERTS, per_expert, 0)

        def tail_copy(c):
            return pltpu.make_async_copy(z_ref, _token_rows(xs_ref, c * td, td), zsem)

        end = pst_ref[N_EXPERTS - 1] + pcnt_ref[N_EXPERTS - 1]
        first_chunk = end // td
        n_chunks = xs_ref.shape[0] // (td * SUBLANES)

        def tail_start(c, carry):
            tail_copy(c).start()
            return carry

        def tail_wait(c, carry):
            tail_copy(c).wait()
            return carry

        lax.fori_loop(first_chunk, n_chunks, tail_start, 0)
        lax.fori_loop(first_chunk, n_chunks, tail_wait, 0)


def _dispatch(dest_flat, counts, pcounts, pstarts, h2t, R, td=256):
    T = h2t.shape[0] // SUBLANES
    return pl.pallas_call(
        functools.partial(_dispatch_kernel, td=td),
        grid_spec=pltpu.PrefetchScalarGridSpec(
            num_scalar_prefetch=4,
            grid=(T // td,),
            in_specs=[pl.BlockSpec((td * SUBLANES, LANES), lambda i, *_: (i, 0))],
            out_specs=pl.BlockSpec(memory_space=pl.ANY),
            scratch_shapes=[
                pltpu.VMEM((td * SUBLANES, LANES), h2t.dtype),
                pltpu.SemaphoreType.DMA,
                pltpu.SemaphoreType.DMA,
            ],
        ),
        out_shape=jax.ShapeDtypeStruct((R * SUBLANES, LANES), h2t.dtype),
        compiler_params=_cparams(("arbitrary",)),
        name="dispatch",
    )(dest_flat, counts, pcounts, pstarts, h2t)


def _experts_kernel(be_ref, used_ref, xs_ref, wgu_ref, bgu_ref, wd_ref, bd_ref, o_ref,
                    wgu_bf, wd_bf, *, n_chunk):
    i = pl.program_id(0)
    prev = be_ref[jnp.maximum(i - 1, 0)]
    new_expert = jnp.logical_or(i == 0, be_ref[i] != prev)

    @pl.when(new_expert)
    def _():
        wgu_bf[...] = wgu_ref[0].astype(jnp.bfloat16)
        wd_bf[...] = wd_ref[0].astype(jnp.bfloat16)

    blk = xs_ref.shape[0] // SUBLANES
    n_lane_chunks = D_MODEL // LANES

    @pl.when(i < used_ref[0])
    def _():
        xb = jnp.concatenate(
            [xs_ref[pl.ds(c, blk, stride=SUBLANES), :].astype(jnp.bfloat16)
             for c in range(n_lane_chunks)], axis=1)
        acc = jnp.zeros((blk, D_MODEL), jnp.float32)
        for c in range(D_FF // n_chunk):
            gs = slice(c * n_chunk, (c + 1) * n_chunk)
            us = slice(D_FF + c * n_chunk, D_FF + (c + 1) * n_chunk)
            g = jnp.dot(xb, wgu_bf[:, gs], preferred_element_type=jnp.float32) + bgu_ref[0, :, gs]
            u = jnp.dot(xb, wgu_bf[:, us], preferred_element_type=jnp.float32) + bgu_ref[0, :, us]
            g = jnp.minimum(g, SWIGLU_LIMIT)
            u = jnp.clip(u, -SWIGLU_LIMIT, SWIGLU_LIMIT)
            act = (u + 1.0) * (g * jax.nn.sigmoid(SWIGLU_ALPHA * g))
            acc = acc + jnp.dot(act.astype(jnp.bfloat16), wd_bf[gs, :],
                                preferred_element_type=jnp.float32)
        res = acc + bd_ref[0]
        for c in range(n_lane_chunks):
            o_ref[pl.ds(c, blk, stride=SUBLANES), :] = res[:, c * LANES:(c + 1) * LANES]

    @pl.when(i >= used_ref[0])
    def _():
        o_ref[...] = jnp.zeros(o_ref.shape, o_ref.dtype)


def _experts(block_e, used, xs, w_gate_up, b_gate_up, w_down, b_down, blk=MOE_BLK):
    R = xs.shape[0] // SUBLANES
    D = D_MODEL
    n_blocks = R // blk
    F2 = w_gate_up.shape[2]

    def row_map(i, be, used):
        return (i, 0)

    return pl.pallas_call(
        functools.partial(_experts_kernel, n_chunk=512),
        grid_spec=pltpu.PrefetchScalarGridSpec(
            num_scalar_prefetch=2,
            grid=(n_blocks,),
            in_specs=[
                pl.BlockSpec((blk * SUBLANES, LANES), row_map),
                pl.BlockSpec((1, D, F2), lambda i, be, used: (be[i], 0, 0)),
                pl.BlockSpec((1, 1, F2), lambda i, be, used: (be[i], 0, 0)),
                pl.BlockSpec((1, D_FF, D), lambda i, be, used: (be[i], 0, 0)),
                pl.BlockSpec((1, 1, D), lambda i, be, used: (be[i], 0, 0)),
            ],
            out_specs=pl.BlockSpec((blk * SUBLANES, LANES), row_map),
            scratch_shapes=[
                pltpu.VMEM((D, F2), jnp.bfloat16),
                pltpu.VMEM((D_FF, D), jnp.bfloat16),
            ],
        ),
        out_shape=jax.ShapeDtypeStruct((R * SUBLANES, LANES), jnp.float32),
        compiler_params=_cparams(("arbitrary",)),
        name="experts",
    )(block_e, used, xs, w_gate_up, b_gate_up.reshape(N_EXPERTS, 1, F2),
      w_down, b_down.reshape(N_EXPERTS, 1, D))


def _combine_kernel(dest_ref, x1_ref, g_ref, nw_ref, rows_ref, o_ref, buf_ref, sems, *, tc):
    i = pl.program_id(0)
    n = pl.num_programs(0)
    slot = i & 1

    def issue_tile(tile, s):
        def issue(t, carry):
            for k in range(TOP_K):
                d = dest_ref[(tile * tc + t) * TOP_K + k]
                pltpu.make_async_copy(_token_rows(rows_ref, d), _token_rows(buf_ref.at[s, k], t),
                                      sems.at[s]).start(priority=k % 2)
            return carry

        lax.fori_loop(0, tc, issue, 0)

    @pl.when(i == 0)
    def _():
        issue_tile(0, 0)

    for s in range(2):
        @pl.when((i + 1 < n) & (slot == 1 - s))
        def _(s=s):
            issue_tile(i + 1, s)

    for k in range(TOP_K):
        pltpu.make_async_copy(_token_rows(rows_ref, 0, tc), buf_ref.at[slot, k],
                              sems.at[slot]).wait()

    g = g_ref[...]
    x1 = x1_ref[...]
    chunks = []
    for c in range(D_MODEL // LANES):
        yc = x1[:, c * LANES:(c + 1) * LANES]
        for k in range(TOP_K):
            yc = yc + g[:, k:k + 1] * buf_ref[slot, k, pl.ds(c, tc, stride=SUBLANES), :]
        chunks.append(yc)
    y = jnp.concatenate(chunks, axis=1)
    ms = jnp.mean(y * y, axis=-1, keepdims=True)
    o_ref[...] = (y * lax.rsqrt(ms + NORM_EPS)) * nw_ref[...]


def _combine(dest_flat, x1, gates, nw, rows, tc=128):
    T, D = x1.shape
    return pl.pallas_call(
        functools.partial(_combine_kernel, tc=tc),
        grid_spec=pltpu.PrefetchScalarGridSpec(
            num_scalar_prefetch=1,
            grid=(T // tc,),
            in_specs=[
                pl.BlockSpec((tc, D), lambda i, *_: (i, 0)),
                pl.BlockSpec((tc, LANES), lambda i, *_: (i, 0)),
                pl.BlockSpec((1, D), lambda i, *_: (0, 0)),
                pl.BlockSpec(memory_space=pl.ANY),
            ],
            out_specs=pl.BlockSpec((tc, D), lambda i, *_: (i, 0)),
            scratch_shapes=[
                pltpu.VMEM((2, TOP_K, tc * SUBLANES, LANES), jnp.float32),
                pltpu.SemaphoreType.DMA((2,)),
            ],
        ),
        out_shape=jax.ShapeDtypeStruct((T, D), jnp.float32),
        compiler_params=_cparams(("arbitrary",)),
        name="combine",
    )(dest_flat, x1, gates, nw, rows)


def kernel(x, attn_norm_w, w_in, diff_lambda_q1, diff_lambda_k1, diff_lambda_q2, diff_lambda_k2,
           diff_subln_w, w_out, ffn_norm_w, router_w, router_b, w_gate_up, b_gate_up, w_down,
           b_down, final_norm_w):
    B, S, D = x.shape
    T = B * S
    f32 = jnp.float32
    n = jnp.arange(1, N_HEADS_TOTAL + 1, dtype=f32)
    slopes = jnp.exp2(-8.0 * n / N_HEADS_TOTAL)
    diff_slopes = slopes[0::2] * LOG2E
    dil_slopes = slopes[1::2] * LOG2E
    depth = attn_norm_w.shape[0]
    assert depth == 1, "the combine kernel fuses the final norm, so exactly one layer is supported"
    x2d = x.reshape(T, D)
    diff_w = N_DIFF_HEADS * HEAD_DIM
    colscale = jnp.concatenate([
        jnp.full((diff_w,), DIFF_QK_DIM ** -0.5 * LOG2E, f32), jnp.ones((2 * diff_w,), f32),
        jnp.full((N_DIL_HEADS * HEAD_DIM,), HEAD_DIM ** -0.5 * LOG2E, f32),
        jnp.ones((2 * N_DIL_HEADS * HEAD_DIM,), f32)]).reshape(1, IN_COLS)
    n_blocks = -(-(T * TOP_K + N_EXPERTS * (MOE_BLK - 1)) // MOE_BLK)
    R = n_blocks * MOE_BLK

    for l in range(depth):
        proj = _in_proj(x2d, attn_norm_w[l].reshape(1, D), w_in[l].astype(jnp.bfloat16), colscale)
        lam_init = 0.8 - 0.6 * math.exp(-0.3 * l)
        lam = (jnp.exp(jnp.sum(diff_lambda_q1[l] * diff_lambda_k1[l]).astype(f32))
               - jnp.exp(jnp.sum(diff_lambda_q2[l] * diff_lambda_k2[l]).astype(f32))
               + lam_init).reshape(1)
        subln2 = jnp.tile(diff_subln_w[l], 2).reshape(1, LANES)
        mix_d = _diff_attn(proj, diff_slopes, lam, subln2, B, S, 1.0 - lam_init)
        mix_l = _dil_attn(proj, dil_slopes, B, S)

        rw = jnp.zeros((D, LANES), f32).at[:, :N_EXPERTS].set(router_w[l])
        rw_hi = rw.astype(jnp.bfloat16)
        rw_lo = (rw - rw_hi.astype(f32)).astype(jnp.bfloat16)
        rb = jnp.zeros((1, LANES), f32).at[0, :N_EXPERTS].set(router_b[l])
        x1, h2, route, gates, cnt = _post_attn(
            x2d, mix_d, mix_l, w_out[l].astype(jnp.bfloat16), ffn_norm_w[l].reshape(1, D),
            jnp.concatenate([rw_hi, rw_lo], axis=1), rb)

        counts = cnt[0, :N_EXPERTS]
        pcounts = (counts + MOE_BLK - 1) // MOE_BLK * MOE_BLK
        pends = jnp.cumsum(pcounts)
        pstarts = pends - pcounts
        top_i = route[:, :TOP_K]
        rank = route[:, TOP_K:2 * TOP_K]
        onehot = top_i[..., None] == jnp.arange(N_EXPERTS, dtype=jnp.int32)
        dest = jnp.sum(jnp.where(onehot, pstarts.astype(jnp.int32), 0), axis=-1) + rank
        dest_flat = dest.reshape(T * TOP_K).astype(jnp.int32)
        used = (pends[-1] // MOE_BLK).astype(jnp.int32)
        blk_start = jnp.arange(n_blocks, dtype=jnp.int32) * MOE_BLK
        be = jnp.minimum(jnp.sum(pends[None, :] <= blk_start[:, None], axis=1), N_EXPERTS - 1)
        be_last = jnp.max(jnp.where(jnp.arange(n_blocks) < used, be, 0))
        block_e = jnp.where(jnp.arange(n_blocks) < used, be, be_last).astype(jnp.int32)

        xs = _dispatch(dest_flat, counts.astype(jnp.int32), pcounts.astype(jnp.int32),
                       pstarts.astype(jnp.int32), h2, R)
        rows = _experts(block_e, used.reshape(1), xs, w_gate_up[l], b_gate_up[l], w_down[l], b_down[l])
        x2d = _combine(dest_flat, x1, gates, final_norm_w.reshape(1, D), rows)
    return x2d.reshape(B, S, D)
```

```python
import functools
import math

import jax
import jax.numpy as jnp
from jax import lax
from jax.experimental import pallas as pl
from jax.experimental.pallas import tpu as pltpu

D_MODEL = 1024
HEAD_DIM = 64
N_DIFF_HEADS = 8
DIFF_QK_DIM = 32
N_DIL_HEADS = 8
N_HEADS_TOTAL = 16
DIL_PATTERNS = ((128, 1), (512, 4), (2048, 16))
DIL_W = 128
N_EXPERTS = 32
TOP_K = 4
D_FF = D_MODEL
SWIGLU_LIMIT = 7.0
SWIGLU_ALPHA = 1.702
NORM_EPS = 1e-5
IN_COLS = 3072
LANES = 128
SUBLANES = 8
DIL_UNROLL = 8
LOG2E = math.log2(math.e)
MOE_BLK = 512
VMEM_LIMIT = 56 * 1024 * 1024

NEG_INF = float("-inf")


def _cparams(sem, vmem=VMEM_LIMIT):
    return pltpu.CompilerParams(dimension_semantics=sem, vmem_limit_bytes=vmem)


def _in_proj_kernel(x_ref, nw_ref, w_ref, cs_ref, o_ref, *, n_chunk):
    xf = x_ref[...]
    ms = jnp.mean(xf * xf, axis=-1, keepdims=True)
    h = (xf * lax.rsqrt(ms + NORM_EPS)) * nw_ref[...]
    hb = h.astype(jnp.bfloat16)
    n_cols = o_ref.shape[1]
    for c in range(n_cols // n_chunk):
        sl = slice(c * n_chunk, (c + 1) * n_chunk)
        r = jnp.dot(hb, w_ref[:, sl], preferred_element_type=jnp.float32)
        o_ref[:, sl] = (r * cs_ref[:, sl]).astype(o_ref.dtype)


def _in_proj(x2d, nw, w_bf16, colscale, tm=512):
    T, D = x2d.shape
    N = w_bf16.shape[1]
    return pl.pallas_call(
        functools.partial(_in_proj_kernel, n_chunk=512),
        grid=(T // tm,),
        in_specs=[
            pl.BlockSpec((tm, D), lambda i: (i, 0)),
            pl.BlockSpec((1, D), lambda i: (0, 0)),
            pl.BlockSpec((D, N), lambda i: (0, 0)),
            pl.BlockSpec((1, N), lambda i: (0, 0)),
        ],
        out_specs=pl.BlockSpec((tm, N), lambda i: (i, 0)),
        out_shape=jax.ShapeDtypeStruct((T, N), jnp.bfloat16),
        compiler_params=_cparams(("arbitrary",)),
        name="in_proj",
    )(x2d, nw, w_bf16, colscale)


def _diff_attn_kernel(slopes_ref, lam_ref, q_ref, k_ref, v_ref, w_ref, o_ref,
                      q4t_ref, pos_ref, vt_ref, mask_ref, s_ref, mt_ref, m_ref, acc_ref,
                      *, tq, tk, post_scale):
    assert tq == tk
    p = pl.program_id(1)
    qi = pl.program_id(2)
    n_lanes = 4 * tq
    n_kv = v_ref.shape[0] // tk
    vt_rows = vt_ref.shape[2]

    lane = lax.broadcasted_iota(jnp.int32, (1, n_lanes), 1)
    slope_l = jnp.where(lane < 2 * tq, slopes_ref[2 * p], slopes_ref[2 * p + 1])

    @pl.when(qi == 0)
    def _():
        ones = jnp.ones((vt_rows - HEAD_DIM, tk), jnp.bfloat16)
        for jj in range(n_kv):
            vt = v_ref[jj * tk:(jj + 1) * tk, :].astype(jnp.float32).T.astype(jnp.bfloat16)
            for h in range(2):
                vt_ref[jj, h, 0:HEAD_DIM, :] = vt[h * HEAD_DIM:(h + 1) * HEAD_DIM, :]
                vt_ref[jj, h, HEAD_DIM:vt_rows, :] = ones
        hi = slope_l.astype(jnp.bfloat16).astype(jnp.float32)
        mid = (slope_l - hi).astype(jnp.bfloat16).astype(jnp.float32)
        lo = slope_l - hi - mid
        r16 = lax.broadcasted_iota(jnp.int32, (16, n_lanes), 0)
        rows = jnp.where(r16 == 0, hi, jnp.where(r16 == 1, mid, jnp.where(r16 == 2, lo, 0.0)))
        q4t_ref[LANES:LANES + 16, :] = rows.astype(jnp.bfloat16)
        q4t_ref[LANES + 16:, :] = jnp.zeros((LANES - 16, n_lanes), jnp.bfloat16)
        key_c = lax.broadcasted_iota(jnp.int32, (tk, LANES), 0)
        col_c = lax.broadcasted_iota(jnp.int32, (tk, LANES), 1)
        pos_ref[...] = jnp.where(col_c < 3, key_c, 0).astype(jnp.float32).astype(jnp.bfloat16)
        key_i = lax.broadcasted_iota(jnp.int32, (tk, n_lanes), 0)
        qry_i = lax.broadcasted_iota(jnp.int32, (tk, n_lanes), 1) & (tq - 1)
        mask_ref[0] = jnp.zeros((tk, n_lanes), jnp.float32)
        mask_ref[1] = jnp.where(key_i <= qry_i, 0.0, NEG_INF)

    qt = q_ref[...].astype(jnp.float32).T
    sub = lax.broadcasted_iota(jnp.int32, (LANES, tq), 0) >> 5
    for c in range(4):
        q4t_ref[0:LANES, c * tq:(c + 1) * tq] = jnp.where(sub == c, qt, 0.0).astype(jnp.bfloat16)
    acc_ref[...] = jnp.zeros(acc_ref.shape, jnp.float32)

    def tile_off(n):
        return slope_l * jnp.asarray(n * tk).astype(jnp.float32)

    def scores(n, slot, mask):
        start = pl.multiple_of(n * tk, tk)
        ka = jnp.concatenate([k_ref[pl.ds(start, tk), :], pos_ref[...]], axis=1)
        s = jnp.dot(ka, q4t_ref[...], preferred_element_type=jnp.float32)
        if mask == "diag":
            s = s + mask_ref[1]
        elif mask == "maybe":
            s = s + mask_ref[(n == qi).astype(jnp.int32)]
        s_ref[slot] = s
        mt_ref[slot] = jnp.max(s, axis=0, keepdims=True) + tile_off(n)

    def weights(n, slot, m_prev):
        m_new = jnp.maximum(m_prev, mt_ref[slot])
        alpha = jnp.exp2(m_prev - m_new)
        pb = jnp.exp2(s_ref[slot] - (m_new - tile_off(n))).astype(jnp.bfloat16)
        for h in range(2):
            cols = slice(h * 2 * tq, (h + 1) * 2 * tq)
            pv = jnp.dot(vt_ref[n, h], pb[:, cols], preferred_element_type=jnp.float32)
            acc_ref[h] = alpha[:, cols] * acc_ref[h] + pv
        return m_new

    scores(0, 0, "maybe")

    def pair(u, m):
        scores(2 * u + 1, 1, None)
        m = weights(2 * u, 0, m)
        scores(2 * u + 2, 0, "maybe")
        return weights(2 * u + 1, 1, m)

    m_ref[...] = lax.fori_loop(0, qi // 2, pair, jnp.full((1, n_lanes), NEG_INF, jnp.float32))

    @pl.when((qi & 1) == 1)
    def _():
        scores(qi, 1, "diag")
        weights(qi, 1, weights(qi - 1, 0, m_ref[...]))

    @pl.when((qi & 1) == 0)
    def _():
        weights(qi, 0, m_ref[...])

    lam = lam_ref[0]
    ys = []
    for h in range(2):
        acc = acc_ref[h]
        num = acc[0:HEAD_DIM]
        l = acc[HEAD_DIM:HEAD_DIM + 1]
        o = num[:, :tq] / l[:, :tq] - lam * (num[:, tq:] / l[:, tq:])
        ms = jnp.mean(o * o, axis=0, keepdims=True)
        ys.append(o * lax.rsqrt(ms + NORM_EPS))
    y = jnp.concatenate(ys, axis=0).T
    o_ref[...] = ((y * w_ref[...]) * post_scale).astype(o_ref.dtype)


def _diff_attn(proj, slopes, lam, subln_w2, B, S, post_scale, tq=256, tk=256):
    T = proj.shape[0]
    nq = S // tq
    n_pairs = N_DIFF_HEADS // 2
    kern = functools.partial(_diff_attn_kernel, tq=tq, tk=tk, post_scale=post_scale)
    return pl.pallas_call(
        kern,
        grid=(B, n_pairs, nq),
        in_specs=[
            pl.BlockSpec(memory_space=pltpu.SMEM),
            pl.BlockSpec(memory_space=pltpu.SMEM),
            pl.BlockSpec((tq, LANES), lambda b, p, i: (b * nq + i, p)),
            pl.BlockSpec((S, LANES), lambda b, p, i: (b, n_pairs + p)),
            pl.BlockSpec((S, LANES), lambda b, p, i: (b, 2 * n_pairs + p)),
            pl.BlockSpec((1, LANES), lambda b, p, i: (0, 0)),
        ],
        out_specs=pl.BlockSpec((tq, LANES), lambda b, p, i: (b * nq + i, p)),
        out_shape=jax.ShapeDtypeStruct((T, N_DIFF_HEADS * HEAD_DIM), jnp.bfloat16),
        scratch_shapes=[
            pltpu.VMEM((2 * LANES, 4 * tq), jnp.bfloat16),
            pltpu.VMEM((tk, LANES), jnp.bfloat16),
            pltpu.VMEM((S // tk, 2, HEAD_DIM + 16, tk), jnp.bfloat16),
            pltpu.VMEM((2, tk, 4 * tq), jnp.float32),
            pltpu.VMEM((2, tk, 4 * tq), jnp.float32),
            pltpu.VMEM((2, 1, 4 * tq), jnp.float32),
            pltpu.VMEM((1, 4 * tq), jnp.float32),
            pltpu.VMEM((2, HEAD_DIM + 16, 2 * tq), jnp.float32),
        ],
        compiler_params=_cparams(("arbitrary", "arbitrary", "arbitrary")),
        name="diff_attn",
    )(slopes, lam, proj, proj, proj, subln_w2)


def _dil_attn_kernel(slopes_ref, q_ref, k_ref, v_ref, o_ref,
                     qf_ref, kf_ref, vf_ref, qd_ref, kd_ref, vd_ref, a_ref,
                     po_ref, pl_ref, ro_ref, rl_ref, *, S):
    p = pl.program_id(1)
    W = DIL_W
    n_blocks = S // W
    qf_ref[...] = q_ref[...].astype(jnp.float32)
    kf_ref[...] = k_ref[...].astype(jnp.float32)
    vf_ref[...] = v_ref[...].astype(jnp.float32)
    @pl.when((pl.program_id(0) == 0) & (p == 0))
    def _():
        kd_ref[0:W, :] = jnp.zeros((W, LANES), jnp.bfloat16)
        vd_ref[0:W, 0:LANES] = jnp.zeros((W, LANES), jnp.bfloat16)
        vd_ref[:, LANES:] = jnp.ones((S + W, LANES), jnp.bfloat16)

    lane = lax.broadcasted_iota(jnp.int32, (W, LANES), 1)
    first = lane < HEAD_DIM
    r_i = lax.broadcasted_iota(jnp.int32, (W, 2 * W), 0)
    c_i = lax.broadcasted_iota(jnp.int32, (W, 2 * W), 1)
    dist = W + r_i - c_i
    valid = (dist >= 0) & (dist <= W)
    dist_f = dist.astype(jnp.float32)

    for pi, (_, d) in enumerate(DIL_PATTERNS):
        L = S // d
        nbk = L // W
        if d == 1:
            qd_ref[...] = q_ref[...]
            kd_ref[W:W + S, :] = k_ref[...]
            vd_ref[W:W + S, 0:LANES] = v_ref[...]
        else:
            for r in range(d):
                qd_ref[r * L:(r + 1) * L, :] = qf_ref[pl.ds(r, L, stride=d), :].astype(jnp.bfloat16)
                kd_ref[W + r * L:W + (r + 1) * L, :] = kf_ref[pl.ds(r, L, stride=d), :].astype(jnp.bfloat16)
                vd_ref[W + r * L:W + (r + 1) * L, 0:LANES] = vf_ref[pl.ds(r, L, stride=d), :].astype(jnp.bfloat16)
        for h in range(2):
            slope = slopes_ref[2 * p + h] * float(d)
            full = jnp.where(valid, slope * dist_f, jnp.inf)
            a_ref[0, h * W:(h + 1) * W, :] = full
            a_ref[1, h * W:(h + 1) * W, :] = jnp.where(c_i < W, jnp.inf, full)

        dst_o = ro_ref.at[pi] if d == 1 else po_ref
        dst_l = rl_ref.at[pi] if d == 1 else pl_ref

        def one_block(u, no_prev, dst_o=dst_o, dst_l=dst_l):
            row0 = pl.multiple_of(u * W, W)
            qb = qd_ref[pl.ds(row0, W), :].astype(jnp.float32)
            kk = kd_ref[pl.ds(row0, 2 * W), :]
            vv = vd_ref[pl.ds(row0, 2 * W), :]
            q2 = jnp.concatenate([jnp.where(first, qb, 0.0), jnp.where(first, 0.0, qb)],
                                 axis=0).astype(jnp.bfloat16)
            s = lax.dot_general(q2, kk, (((1,), (1,)), ((), ())),
                                preferred_element_type=jnp.float32)
            s = s - a_ref[no_prev]
            m = jnp.max(s, axis=1, keepdims=True)
            pr = jnp.exp2(s - m)
            pv = jnp.dot(pr.astype(jnp.bfloat16), vv, preferred_element_type=jnp.float32)
            l = pv[:, LANES:]
            o = pv[:, :LANES] / l
            lse = m + jnp.log2(l)
            dst_o[pl.ds(row0, W), :] = jnp.where(first, o[:W], o[W:])
            dst_l[pl.ds(row0, W), :] = jnp.where(first, lse[:W], lse[W:])

        def group(g, carry, nbk=nbk):
            for i in range(DIL_UNROLL):
                u = g * DIL_UNROLL + i
                if DIL_UNROLL % nbk == 0:
                    no_prev = int(i % nbk == 0)
                elif i > 0:
                    no_prev = 0
                else:
                    no_prev = (lax.rem(u, nbk) == 0).astype(jnp.int32)
                one_block(u, no_prev)
            return carry

        assert DIL_UNROLL % nbk == 0 or nbk % DIL_UNROLL == 0
        lax.fori_loop(0, n_blocks // DIL_UNROLL, group, 0)

        if d > 1:
            for r in range(d):
                ro_ref[pi, pl.ds(r, L, stride=d), :] = po_ref[r * L:(r + 1) * L, :]
                rl_ref[pi, pl.ds(r, L, stride=d), :] = pl_ref[r * L:(r + 1) * L, :]

    def mix(c, carry):
        row0 = pl.multiple_of(c * W, W)
        ls = [rl_ref[i, pl.ds(row0, W), :] for i in range(3)]
        mx = jnp.maximum(jnp.maximum(ls[0], ls[1]), ls[2])
        es = [jnp.exp2(x - mx) for x in ls]
        den = es[0] + es[1] + es[2]
        num = (es[0] * ro_ref[0, pl.ds(row0, W), :] + es[1] * ro_ref[1, pl.ds(row0, W), :]
               + es[2] * ro_ref[2, pl.ds(row0, W), :])
        o_ref[pl.ds(row0, W), :] = (num / den).astype(o_ref.dtype)
        return carry

    lax.fori_loop(0, n_blocks, mix, 0)


def _dil_attn(proj, slopes, B, S):
    T = proj.shape[0]
    n_pairs = N_DIL_HEADS // 2
    base = 3 * N_DIFF_HEADS * HEAD_DIM // LANES
    W = DIL_W
    return pl.pallas_call(
        functools.partial(_dil_attn_kernel, S=S),
        grid=(B, n_pairs),
        in_specs=[
            pl.BlockSpec(memory_space=pltpu.SMEM),
            pl.BlockSpec((S, LANES), lambda b, p: (b, base + p)),
            pl.BlockSpec((S, LANES), lambda b, p: (b, base + n_pairs + p)),
            pl.BlockSpec((S, LANES), lambda b, p: (b, base + 2 * n_pairs + p)),
        ],
        out_specs=pl.BlockSpec((S, LANES), lambda b, p: (b, p)),
        out_shape=jax.ShapeDtypeStruct((T, N_DIL_HEADS * HEAD_DIM), jnp.bfloat16),
        scratch_shapes=[
            pltpu.VMEM((S, LANES), jnp.float32),
            pltpu.VMEM((S, LANES), jnp.float32),
            pltpu.VMEM((S, LANES), jnp.float32),
            pltpu.VMEM((S, LANES), jnp.bfloat16),
            pltpu.VMEM((S + W, LANES), jnp.bfloat16),
            pltpu.VMEM((S + W, 2 * LANES), jnp.bfloat16),
            pltpu.VMEM((2, 2 * W, 2 * W), jnp.float32),
            pltpu.VMEM((S, LANES), jnp.float32),
            pltpu.VMEM((S, LANES), jnp.float32),
            pltpu.VMEM((3, S, LANES), jnp.float32),
            pltpu.VMEM((3, S, LANES), jnp.float32),
        ],
        compiler_params=_cparams(("arbitrary", "arbitrary")),
        name="dil_attn",
    )(slopes, proj, proj, proj)


def _post_attn_kernel(x_ref, md_ref, ml_ref, wo_ref, nw_ref, rw_ref, rb_ref,
                      x1_ref, h2_ref, route_ref, gate_ref, cnt_ref, run_ref, tri_ref, *, tm):
    i = pl.program_id(0)

    @pl.when(i == 0)
    def _():
        run_ref[...] = jnp.zeros(run_ref.shape, jnp.float32)
        r_i = lax.broadcasted_iota(jnp.int32, (tm, tm), 0)
        c_i = lax.broadcasted_iota(jnp.int32, (tm, tm), 1)
        tri_ref[...] = jnp.where(c_i < r_i, 1.0, 0.0).astype(jnp.bfloat16)

    half = md_ref.shape[1]
    y = jnp.dot(md_ref[...], wo_ref[0:half, :], preferred_element_type=jnp.float32)
    y = y + jnp.dot(ml_ref[...], wo_ref[half:, :], preferred_element_type=jnp.float32)
    x1 = x_ref[...] + y
    x1_ref[...] = x1
    ms = jnp.mean(x1 * x1, axis=-1, keepdims=True)
    h2 = (x1 * lax.rsqrt(ms + NORM_EPS)) * nw_ref[...]
    for c in range(D_MODEL // LANES):
        h2_ref[pl.ds(c, tm, stride=SUBLANES), :] = h2[:, c * LANES:(c + 1) * LANES]
    hh = h2.astype(jnp.bfloat16)
    hl = (h2 - hh.astype(jnp.float32)).astype(jnp.bfloat16)
    lh = jnp.dot(hh, rw_ref[...], preferred_element_type=jnp.float32)
    ll = jnp.dot(hl, rw_ref[...], preferred_element_type=jnp.float32)
    logits = (lh[:, :LANES] + lh[:, LANES:]) + (ll[:, :LANES] + ll[:, LANES:]) + rb_ref[...]
    lane = lax.broadcasted_iota(jnp.int32, (tm, LANES), 1)
    work = jnp.where(lane < N_EXPERTS, logits, NEG_INF)
    vals, idxs, sels = [], [], []
    for _ in range(TOP_K):
        mk = jnp.max(work, axis=1, keepdims=True)
        ik = jnp.min(jnp.where(work == mk, lane, LANES), axis=1, keepdims=True)
        sel = lane == ik
        work = jnp.where(sel, NEG_INF, work)
        vals.append(mk)
        idxs.append(ik)
        sels.append(sel)
    es = [jnp.exp(v - vals[0]) for v in vals]
    den = es[0] + es[1] + es[2] + es[3]
    member = jnp.zeros((tm, LANES), jnp.float32)
    for sel in sels:
        member = jnp.where(sel, 1.0, member)
    before = jnp.dot(tri_ref[...], member.astype(jnp.bfloat16), preferred_element_type=jnp.float32)
    before = before + run_ref[...]
    route = jnp.zeros((tm, LANES), jnp.int32)
    gates = jnp.zeros((tm, LANES), jnp.float32)
    for k in range(TOP_K):
        rank = jnp.sum(jnp.where(sels[k], before, 0.0), axis=1, keepdims=True)
        route = jnp.where(lane == k, idxs[k], route)
        route = jnp.where(lane == TOP_K + k, rank.astype(jnp.int32), route)
        gates = jnp.where(lane == k, es[k] / den, gates)
    route_ref[...] = route
    gate_ref[...] = gates
    run_new = run_ref[...] + jnp.sum(member, axis=0, keepdims=True)
    run_ref[...] = run_new
    cnt_ref[...] = jnp.broadcast_to(run_new, cnt_ref.shape).astype(jnp.int32)


def _post_attn(x2d, mix_d, mix_l, wo_bf16, nw, rw_cat, rb, tm=512):
    T, D = x2d.shape
    half = mix_d.shape[1]
    return pl.pallas_call(
        functools.partial(_post_attn_kernel, tm=tm),
        grid=(T // tm,),
        in_specs=[
            pl.BlockSpec((tm, D), lambda i: (i, 0)),
            pl.BlockSpec((tm, half), lambda i: (i, 0)),
            pl.BlockSpec((tm, half), lambda i: (i, 0)),
            pl.BlockSpec((D, D), lambda i: (0, 0)),
            pl.BlockSpec((1, D), lambda i: (0, 0)),
            pl.BlockSpec((D, 2 * LANES), lambda i: (0, 0)),
            pl.BlockSpec((1, LANES), lambda i: (0, 0)),
        ],
        out_specs=[
            pl.BlockSpec((tm, D), lambda i: (i, 0)),
            pl.BlockSpec((tm * SUBLANES, LANES), lambda i: (i, 0)),
            pl.BlockSpec((tm, LANES), lambda i: (i, 0)),
            pl.BlockSpec((tm, LANES), lambda i: (i, 0)),
            pl.BlockSpec((8, LANES), lambda i: (0, 0)),
        ],
        out_shape=[
            jax.ShapeDtypeStruct((T, D), jnp.float32),
            jax.ShapeDtypeStruct((T * SUBLANES, LANES), jnp.float32),
            jax.ShapeDtypeStruct((T, LANES), jnp.int32),
            jax.ShapeDtypeStruct((T, LANES), jnp.float32),
            jax.ShapeDtypeStruct((8, LANES), jnp.int32),
        ],
        scratch_shapes=[pltpu.VMEM((1, LANES), jnp.float32), pltpu.VMEM((tm, tm), jnp.bfloat16)],
        compiler_params=_cparams(("arbitrary",)),
        name="post_attn",
    )(x2d, mix_d, mix_l, wo_bf16, nw, rw_cat, rb)


def _token_rows(ref, t, n_tokens=1):
    start = pl.multiple_of(t * SUBLANES, SUBLANES)
    return ref.at[pl.ds(start, n_tokens * SUBLANES), :]


def _dispatch_kernel(dest_ref, cnt_ref, pcnt_ref, pst_ref, h_ref, xs_ref, z_ref, sem, zsem, *, td):
    i = pl.program_id(0)
    n = pl.num_programs(0)

    def issue(t, carry):
        tok = i * td + t
        for k in range(TOP_K):
            pltpu.make_async_copy(_token_rows(h_ref, tok),
                                  _token_rows(xs_ref, dest_ref[tok * TOP_K + k]),
                                  sem).start(priority=k % 2)
        return carry

    lax.fori_loop(0, td, issue, 0)

    def retire_one_step():
        for k in range(TOP_K):
            pltpu.make_async_copy(_token_rows(h_ref, 0, td), _token_rows(xs_ref, 0, td), sem).wait()

    @pl.when(i > 0)
    def _():
        retire_one_step()

    @pl.when(i == n - 1)
    def _():
        retire_one_step()
        z_ref[...] = jnp.zeros(z_ref.shape, z_ref.dtype)

        def pad_copy(row):
            return pltpu.make_async_copy(_token_rows(z_ref, 0), _token_rows(xs_ref, row), zsem)

        def per_expert(e, carry):
            base = pst_ref[e]

            def start(r, c):
                pad_copy(base + r).start()
                return c

            def wait(r, c):
                pad_copy(base + r).wait()
                return c

            lax.fori_loop(cnt_ref[e], pcnt_ref[e], start, 0)
            lax.fori_loop(cnt_ref[e], pcnt_ref[e], wait, 0)
            return carry

        lax.fori_loop(0, N_EXPERTS, per_expert, 0)

        def tail_copy(c):
            return pltpu.make_async_copy(z_ref, _token_rows(xs_ref, c * td, td), zsem)

        end = pst_ref[N_EXPERTS - 1] + pcnt_ref[N_EXPERTS - 1]
        first_chunk = end // td
        n_chunks = xs_ref.shape[0] // (td * SUBLANES)

        def tail_start(c, carry):
            tail_copy(c).start()
            return carry

        def tail_wait(c, carry):
            tail_copy(c).wait()
            return carry

        lax.fori_loop(first_chunk, n_chunks, tail_start, 0)
        lax.fori_loop(first_chunk, n_chunks, tail_wait, 0)


def _dispatch(dest_flat, counts, pcounts, pstarts, h2t, R, td=256):
    T = h2t.shape[0] // SUBLANES
    return pl.pallas_call(
        functools.partial(_dispatch_kernel, td=td),
        grid_spec=pltpu.PrefetchScalarGridSpec(
            num_scalar_prefetch=4,
            grid=(T // td,),
            in_specs=[pl.BlockSpec(memory_space=pl.ANY)],
            out_specs=pl.BlockSpec(memory_space=pl.ANY),
            scratch_shapes=[
                pltpu.VMEM((td * SUBLANES, LANES), h2t.dtype),
                pltpu.SemaphoreType.DMA,
                pltpu.SemaphoreType.DMA,
            ],
        ),
        out_shape=jax.ShapeDtypeStruct((R * SUBLANES, LANES), h2t.dtype),
        compiler_params=_cparams(("arbitrary",)),
        name="dispatch",
    )(dest_flat, counts, pcounts, pstarts, h2t)


def _experts_kernel(be_ref, used_ref, xs_ref, wgu_ref, bgu_ref, wd_ref, bd_ref, o_ref,
                    wgu_bf, wd_bf, *, n_chunk):
    i = pl.program_id(0)
    prev = be_ref[jnp.maximum(i - 1, 0)]
    new_expert = jnp.logical_or(i == 0, be_ref[i] != prev)

    @pl.when(new_expert)
    def _():
        wgu_bf[...] = wgu_ref[0].astype(jnp.bfloat16)
        wd_bf[...] = wd_ref[0].astype(jnp.bfloat16)

    blk = xs_ref.shape[0] // SUBLANES
    n_lane_chunks = D_MODEL // LANES

    @pl.when(i < used_ref[0])
    def _():
        xb = jnp.concatenate(
            [xs_ref[pl.ds(c, blk, stride=SUBLANES), :].astype(jnp.bfloat16)
             for c in range(n_lane_chunks)], axis=1)
        acc = jnp.zeros((blk, D_MODEL), jnp.float32)
        for c in range(D_FF // n_chunk):
            gs = slice(c * n_chunk, (c + 1) * n_chunk)
            us = slice(D_FF + c * n_chunk, D_FF + (c + 1) * n_chunk)
            g = jnp.dot(xb, wgu_bf[:, gs], preferred_element_type=jnp.float32) + bgu_ref[0, :, gs]
            u = jnp.dot(xb, wgu_bf[:, us], preferred_element_type=jnp.float32) + bgu_ref[0, :, us]
            g = jnp.minimum(g, SWIGLU_LIMIT)
            u = jnp.clip(u, -SWIGLU_LIMIT, SWIGLU_LIMIT)
            act = (u + 1.0) * (g * jax.nn.sigmoid(SWIGLU_ALPHA * g))
            acc = acc + jnp.dot(act.astype(jnp.bfloat16), wd_bf[gs, :],
                                preferred_element_type=jnp.float32)
        res = acc + bd_ref[0]
        for c in range(n_lane_chunks):
            o_ref[pl.ds(c, blk, stride=SUBLANES), :] = res[:, c * LANES:(c + 1) * LANES]

    @pl.when(i >= used_ref[0])
    def _():
        o_ref[...] = jnp.zeros(o_ref.shape, o_ref.dtype)


def _experts(block_e, used, xs, w_gate_up, b_gate_up, w_down, b_down, blk=MOE_BLK):
    R = xs.shape[0] // SUBLANES
    D = D_MODEL
    n_blocks = R // blk
    F2 = w_gate_up.shape[2]

    def row_map(i, be, used):
        return (i, 0)

    return pl.pallas_call(
        functools.partial(_experts_kernel, n_chunk=512),
        grid_spec=pltpu.PrefetchScalarGridSpec(
            num_scalar_prefetch=2,
            grid=(n_blocks,),
            in_specs=[
                pl.BlockSpec((blk * SUBLANES, LANES), row_map),
                pl.BlockSpec((1, D, F2), lambda i, be, used: (be[i], 0, 0)),
                pl.BlockSpec((1, 1, F2), lambda i, be, used: (be[i], 0, 0)),
                pl.BlockSpec((1, D_FF, D), lambda i, be, used: (be[i], 0, 0)),
                pl.BlockSpec((1, 1, D), lambda i, be, used: (be[i], 0, 0)),
            ],
            out_specs=pl.BlockSpec((blk * SUBLANES, LANES), row_map),
            scratch_shapes=[
                pltpu.VMEM((D, F2), jnp.bfloat16),
                pltpu.VMEM((D_FF, D), jnp.bfloat16),
            ],
        ),
        out_shape=jax.ShapeDtypeStruct((R * SUBLANES, LANES), jnp.float32),
        compiler_params=_cparams(("arbitrary",)),
        name="experts",
    )(block_e, used, xs, w_gate_up, b_gate_up.reshape(N_EXPERTS, 1, F2),
      w_down, b_down.reshape(N_EXPERTS, 1, D))


def _combine_kernel(dest_ref, x1_ref, g_ref, nw_ref, rows_ref, o_ref, buf_ref, sems, *, tc):
    i = pl.program_id(0)
    n = pl.num_programs(0)
    slot = i & 1

    def issue_tile(tile, s):
        def issue(t, carry):
            for k in range(TOP_K):
                d = dest_ref[(tile * tc + t) * TOP_K + k]
                pltpu.make_async_copy(_token_rows(rows_ref, d), _token_rows(buf_ref.at[s, k], t),
                                      sems.at[s]).start(priority=k % 2)
            return carry

        lax.fori_loop(0, tc, issue, 0)

    @pl.when(i == 0)
    def _():
        issue_tile(0, 0)

    for s in range(2):
        @pl.when((i + 1 < n) & (slot == 1 - s))
        def _(s=s):
            issue_tile(i + 1, s)

    for k in range(TOP_K):
        pltpu.make_async_copy(_token_rows(rows_ref, 0, tc), buf_ref.at[slot, k],
                              sems.at[slot]).wait()

    g = g_ref[...]
    x1 = x1_ref[...]
    chunks = []
    for c in range(D_MODEL // LANES):
        yc = x1[:, c * LANES:(c + 1) * LANES]
        for k in range(TOP_K):
            yc = yc + g[:, k:k + 1] * buf_ref[slot, k, pl.ds(c, tc, stride=SUBLANES), :]
        chunks.append(yc)
    y = jnp.concatenate(chunks, axis=1)
    ms = jnp.mean(y * y, axis=-1, keepdims=True)
    o_ref[...] = (y * lax.rsqrt(ms + NORM_EPS)) * nw_ref[...]


def _combine(dest_flat, x1, gates, nw, rows, tc=256):
    T, D = x1.shape
    return pl.pallas_call(
        functools.partial(_combine_kernel, tc=tc),
        grid_spec=pltpu.PrefetchScalarGridSpec(
            num_scalar_prefetch=1,
            grid=(T // tc,),
            in_specs=[
                pl.BlockSpec((tc, D), lambda i, *_: (i, 0)),
                pl.BlockSpec((tc, LANES), lambda i, *_: (i, 0)),
                pl.BlockSpec((1, D), lambda i, *_: (0, 0)),
                pl.BlockSpec(memory_space=pl.ANY),
            ],
            out_specs=pl.BlockSpec((tc, D), lambda i, *_: (i, 0)),
            scratch_shapes=[
                pltpu.VMEM((2, TOP_K, tc * SUBLANES, LANES), jnp.float32),
                pltpu.SemaphoreType.DMA((2,)),
            ],
        ),
        out_shape=jax.ShapeDtypeStruct((T, D), jnp.float32),
        compiler_params=_cparams(("arbitrary",)),
        name="combine",
    )(dest_flat, x1, gates, nw, rows)


def kernel(x, attn_norm_w, w_in, diff_lambda_q1, diff_lambda_k1, diff_lambda_q2, diff_lambda_k2,
           diff_subln_w, w_out, ffn_norm_w, router_w, router_b, w_gate_up, b_gate_up, w_down,
           b_down, final_norm_w):
    B, S, D = x.shape
    T = B * S
    f32 = jnp.float32
    n = jnp.arange(1, N_HEADS_TOTAL + 1, dtype=f32)
    slopes = jnp.exp2(-8.0 * n / N_HEADS_TOTAL)
    diff_slopes = slopes[0::2] * LOG2E
    dil_slopes = slopes[1::2] * LOG2E
    depth = attn_norm_w.shape[0]
    assert depth == 1, "the combine kernel fuses the final norm, so exactly one layer is supported"
    x2d = x.reshape(T, D)
    diff_w = N_DIFF_HEADS * HEAD_DIM
    colscale = jnp.concatenate([
        jnp.full((diff_w,), DIFF_QK_DIM ** -0.5 * LOG2E, f32), jnp.ones((2 * diff_w,), f32),
        jnp.full((N_DIL_HEADS * HEAD_DIM,), HEAD_DIM ** -0.5 * LOG2E, f32),
        jnp.ones((2 * N_DIL_HEADS * HEAD_DIM,), f32)]).reshape(1, IN_COLS)
    n_blocks = -(-(T * TOP_K + N_EXPERTS * (MOE_BLK - 1)) // MOE_BLK)
    R = n_blocks * MOE_BLK

    for l in range(depth):
        proj = _in_proj(x2d, attn_norm_w[l].reshape(1, D), w_in[l].astype(jnp.bfloat16), colscale)
        lam_init = 0.8 - 0.6 * math.exp(-0.3 * l)
        lam = (jnp.exp(jnp.sum(diff_lambda_q1[l] * diff_lambda_k1[l]).astype(f32))
               - jnp.exp(jnp.sum(diff_lambda_q2[l] * diff_lambda_k2[l]).astype(f32))
               + lam_init).reshape(1)
        subln2 = jnp.tile(diff_subln_w[l], 2).reshape(1, LANES)
        mix_d = _diff_attn(proj, diff_slopes, lam, subln2, B, S, 1.0 - lam_init)
        mix_l = _dil_attn(proj, dil_slopes, B, S)

        rw = jnp.zeros((D, LANES), f32).at[:, :N_EXPERTS].set(router_w[l])
        rw_hi = rw.astype(jnp.bfloat16)
        rw_lo = (rw - rw_hi.astype(f32)).astype(jnp.bfloat16)
        rb = jnp.zeros((1, LANES), f32).at[0, :N_EXPERTS].set(router_b[l])
        x1, h2, route, gates, cnt = _post_attn(
            x2d, mix_d, mix_l, w_out[l].astype(jnp.bfloat16), ffn_norm_w[l].reshape(1, D),
            jnp.concatenate([rw_hi, rw_lo], axis=1), rb)

        counts = cnt[0, :N_EXPERTS]
        pcounts = (counts + MOE_BLK - 1) // MOE_BLK * MOE_BLK
        pends = jnp.cumsum(pcounts)
        pstarts = pends - pcounts
        top_i = route[:, :TOP_K]
        rank = route[:, TOP_K:2 * TOP_K]
        onehot = top_i[..., None] == jnp.arange(N_EXPERTS, dtype=jnp.int32)
        dest = jnp.sum(jnp.where(onehot, pstarts.astype(jnp.int32), 0), axis=-1) + rank
        dest_flat = dest.reshape(T * TOP_K).astype(jnp.int32)
        used = (pends[-1] // MOE_BLK).astype(jnp.int32)
        blk_start = jnp.arange(n_blocks, dtype=jnp.int32) * MOE_BLK
        be = jnp.minimum(jnp.sum(pends[None, :] <= blk_start[:, None], axis=1), N_EXPERTS - 1)
        be_last = jnp.max(jnp.where(jnp.arange(n_blocks) < used, be, 0))
        block_e = jnp.where(jnp.arange(n_blocks) < used, be, be_last).astype(jnp.int32)

        xs = _dispatch(dest_flat, counts.astype(jnp.int32), pcounts.astype(jnp.int32),
                       pstarts.astype(jnp.int32), h2, R)
        rows = _experts(block_e, used.reshape(1), xs, w_gate_up[l], b_gate_up[l], w_down[l], b_down[l])
        x2d = _combine(dest_flat, x1, gates, final_norm_w.reshape(1, D), rows)
    return x2d.reshape(B, S, D)
```

```python
import functools
import math

import jax
import jax.numpy as jnp
from jax import lax
from jax.experimental import pallas as pl
from jax.experimental.pallas import tpu as pltpu

D_MODEL = 1024
HEAD_DIM = 64
N_DIFF_HEADS = 8
DIFF_QK_DIM = 32
N_DIL_HEADS = 8
N_HEADS_TOTAL = 16
DIL_PATTERNS = ((128, 1), (512, 4), (2048, 16))
DIL_W = 128
N_EXPERTS = 32
TOP_K = 4
D_FF = D_MODEL
SWIGLU_LIMIT = 7.0
SWIGLU_ALPHA = 1.702
NORM_EPS = 1e-5
IN_COLS = 3072
LANES = 128
SUBLANES = 8
DIL_UNROLL = 8
LOG2E = math.log2(math.e)
MOE_BLK = 512
VMEM_LIMIT = 56 * 1024 * 1024

NEG_INF = float("-inf")


def _cparams(sem, vmem=VMEM_LIMIT):
    return pltpu.CompilerParams(dimension_semantics=sem, vmem_limit_bytes=vmem)


def _in_proj_kernel(x_ref, nw_ref, w_ref, cs_ref, o_ref, *, n_chunk):
    xf = x_ref[...]
    ms = jnp.mean(xf * xf, axis=-1, keepdims=True)
    h = (xf * lax.rsqrt(ms + NORM_EPS)) * nw_ref[...]
    hb = h.astype(jnp.bfloat16)
    n_cols = o_ref.shape[1]
    for c in range(n_cols // n_chunk):
        sl = slice(c * n_chunk, (c + 1) * n_chunk)
        r = jnp.dot(hb, w_ref[:, sl], preferred_element_type=jnp.float32)
        o_ref[:, sl] = (r * cs_ref[:, sl]).astype(o_ref.dtype)


def _in_proj(x2d, nw, w_bf16, colscale, tm=512):
    T, D = x2d.shape
    N = w_bf16.shape[1]
    return pl.pallas_call(
        functools.partial(_in_proj_kernel, n_chunk=512),
        grid=(T // tm,),
        in_specs=[
            pl.BlockSpec((tm, D), lambda i: (i, 0)),
            pl.BlockSpec((1, D), lambda i: (0, 0)),
            pl.BlockSpec((D, N), lambda i: (0, 0)),
            pl.BlockSpec((1, N), lambda i: (0, 0)),
        ],
        out_specs=pl.BlockSpec((tm, N), lambda i: (i, 0)),
        out_shape=jax.ShapeDtypeStruct((T, N), jnp.bfloat16),
        compiler_params=_cparams(("arbitrary",)),
        name="in_proj",
    )(x2d, nw, w_bf16, colscale)


def _diff_attn_kernel(slopes_ref, lam_ref, q_ref, k_ref, v_ref, w_ref, o_ref,
                      q4t_ref, pos_ref, vt_ref, mask_ref, s_ref, mt_ref, acc_ref,
                      *, tq, tk, post_scale):
    assert tq == tk
    p = pl.program_id(1)
    qi = pl.program_id(2)
    n_lanes = 4 * tq
    n_kv = v_ref.shape[0] // tk
    vt_rows = vt_ref.shape[2]

    lane = lax.broadcasted_iota(jnp.int32, (1, n_lanes), 1)
    slope_l = jnp.where(lane < 2 * tq, slopes_ref[2 * p], slopes_ref[2 * p + 1])

    @pl.when(qi == 0)
    def _():
        ones = jnp.ones((vt_rows - HEAD_DIM, tk), jnp.bfloat16)
        for jj in range(n_kv):
            vt = v_ref[jj * tk:(jj + 1) * tk, :].astype(jnp.float32).T.astype(jnp.bfloat16)
            for h in range(2):
                vt_ref[jj, h, 0:HEAD_DIM, :] = vt[h * HEAD_DIM:(h + 1) * HEAD_DIM, :]
                vt_ref[jj, h, HEAD_DIM:vt_rows, :] = ones
        hi = slope_l.astype(jnp.bfloat16).astype(jnp.float32)
        mid = (slope_l - hi).astype(jnp.bfloat16).astype(jnp.float32)
        lo = slope_l - hi - mid
        r16 = lax.broadcasted_iota(jnp.int32, (16, n_lanes), 0)
        rows = jnp.where(r16 == 0, hi, jnp.where(r16 == 1, mid, jnp.where(r16 == 2, lo, 0.0)))
        q4t_ref[LANES:LANES + 16, :] = rows.astype(jnp.bfloat16)
        q4t_ref[LANES + 16:, :] = jnp.zeros((LANES - 16, n_lanes), jnp.bfloat16)
        key_c = lax.broadcasted_iota(jnp.int32, (tk, LANES), 0)
        col_c = lax.broadcasted_iota(jnp.int32, (tk, LANES), 1)
        pos_ref[...] = jnp.where(col_c < 3, key_c, 0).astype(jnp.float32).astype(jnp.bfloat16)
        key_i = lax.broadcasted_iota(jnp.int32, (tk, n_lanes), 0)
        qry_i = lax.broadcasted_iota(jnp.int32, (tk, n_lanes), 1) & (tq - 1)
        mask_ref[...] = jnp.where(key_i <= qry_i, 0.0, NEG_INF)

    qt = q_ref[...].astype(jnp.float32).T
    sub = lax.broadcasted_iota(jnp.int32, (LANES, tq), 0) >> 5
    for c in range(4):
        q4t_ref[0:LANES, c * tq:(c + 1) * tq] = jnp.where(sub == c, qt, 0.0).astype(jnp.bfloat16)
    acc_ref[...] = jnp.zeros(acc_ref.shape, jnp.float32)

    def tile_off(n):
        return slope_l * jnp.asarray(n * tk).astype(jnp.float32)

    def scores(n, slot, mask):
        ka = jnp.concatenate([k_ref[n * tk:(n + 1) * tk, :], pos_ref[...]], axis=1)
        s = jnp.dot(ka, q4t_ref[...], preferred_element_type=jnp.float32)
        if mask == "diag":
            s = s + mask_ref[...]
        s_ref[slot] = s
        mt_ref[slot] = jnp.max(s, axis=0, keepdims=True) + tile_off(n)

    def weights(n, slot, m_prev):
        m_new = jnp.maximum(m_prev, mt_ref[slot])
        alpha = jnp.exp2(m_prev - m_new)
        pb = jnp.exp2(s_ref[slot] - (m_new - tile_off(n))).astype(jnp.bfloat16)
        for h in range(2):
            cols = slice(h * 2 * tq, (h + 1) * 2 * tq)
            pv = jnp.dot(vt_ref[n, h], pb[:, cols], preferred_element_type=jnp.float32)
            acc_ref[h] = alpha[:, cols] * acc_ref[h] + pv
        return m_new

    def run_tiles(last):
        scores(0, 0, "diag" if last == 0 else None)
        m = jnp.full((1, n_lanes), NEG_INF, jnp.float32)
        for t in range(last + 1):
            if t < last:
                scores(t + 1, (t + 1) & 1, "diag" if t + 1 == last else None)
            m = weights(t, t & 1, m)

    for c in range(n_kv):
        pl.when(qi == c)(functools.partial(run_tiles, c))

    lam = lam_ref[0]
    ys = []
    for h in range(2):
        acc = acc_ref[h]
        num = acc[0:HEAD_DIM]
        l = acc[HEAD_DIM:HEAD_DIM + 1]
        o = num[:, :tq] / l[:, :tq] - lam * (num[:, tq:] / l[:, tq:])
        ms = jnp.mean(o * o, axis=0, keepdims=True)
        ys.append(o * lax.rsqrt(ms + NORM_EPS))
    y = jnp.concatenate(ys, axis=0).T
    o_ref[...] = ((y * w_ref[...]) * post_scale).astype(o_ref.dtype)


def _diff_attn(proj, slopes, lam, subln_w2, B, S, post_scale, tq=256, tk=256):
    T = proj.shape[0]
    nq = S // tq
    n_pairs = N_DIFF_HEADS // 2
    kern = functools.partial(_diff_attn_kernel, tq=tq, tk=tk, post_scale=post_scale)
    return pl.pallas_call(
        kern,
        grid=(B, n_pairs, nq),
        in_specs=[
            pl.BlockSpec(memory_space=pltpu.SMEM),
            pl.BlockSpec(memory_space=pltpu.SMEM),
            pl.BlockSpec((tq, LANES), lambda b, p, i: (b * nq + i, p)),
            pl.BlockSpec((S, LANES), lambda b, p, i: (b, n_pairs + p)),
            pl.BlockSpec((S, LANES), lambda b, p, i: (b, 2 * n_pairs + p)),
            pl.BlockSpec((1, LANES), lambda b, p, i: (0, 0)),
        ],
        out_specs=pl.BlockSpec((tq, LANES), lambda b, p, i: (b * nq + i, p)),
        out_shape=jax.ShapeDtypeStruct((T, N_DIFF_HEADS * HEAD_DIM), jnp.bfloat16),
        scratch_shapes=[
            pltpu.VMEM((2 * LANES, 4 * tq), jnp.bfloat16),
            pltpu.VMEM((tk, LANES), jnp.bfloat16),
            pltpu.VMEM((S // tk, 2, HEAD_DIM + 16, tk), jnp.bfloat16),
            pltpu.VMEM((tk, 4 * tq), jnp.float32),
            pltpu.VMEM((2, tk, 4 * tq), jnp.float32),
            pltpu.VMEM((2, 1, 4 * tq), jnp.float32),
            pltpu.VMEM((2, HEAD_DIM + 16, 2 * tq), jnp.float32),
        ],
        compiler_params=_cparams(("arbitrary", "arbitrary", "arbitrary")),
        name="diff_attn",
    )(slopes, lam, proj, proj, proj, subln_w2)


def _dil_attn_kernel(slopes_ref, q_ref, k_ref, v_ref, o_ref,
                     qf_ref, kf_ref, vf_ref, qd_ref, kd_ref, vd_ref, a_ref,
                     po_ref, pl_ref, ro_ref, rl_ref, *, S):
    p = pl.program_id(1)
    W = DIL_W
    n_blocks = S // W
    qf_ref[...] = q_ref[...].astype(jnp.float32)
    kf_ref[...] = k_ref[...].astype(jnp.float32)
    vf_ref[...] = v_ref[...].astype(jnp.float32)
    @pl.when((pl.program_id(0) == 0) & (p == 0))
    def _():
        kd_ref[0:W, :] = jnp.zeros((W, LANES), jnp.bfloat16)
        vd_ref[0:W, 0:LANES] = jnp.zeros((W, LANES), jnp.bfloat16)
        vd_ref[:, LANES:] = jnp.ones((S + W, LANES), jnp.bfloat16)

    lane = lax.broadcasted_iota(jnp.int32, (W, LANES), 1)
    first = lane < HEAD_DIM
    r_i = lax.broadcasted_iota(jnp.int32, (W, 2 * W), 0)
    c_i = lax.broadcasted_iota(jnp.int32, (W, 2 * W), 1)
    dist = W + r_i - c_i
    valid = (dist >= 0) & (dist <= W)
    dist_f = dist.astype(jnp.float32)

    for pi, (_, d) in enumerate(DIL_PATTERNS):
        L = S // d
        nbk = L // W
        if d == 1:
            qd_ref[...] = q_ref[...]
            kd_ref[W:W + S, :] = k_ref[...]
            vd_ref[W:W + S, 0:LANES] = v_ref[...]
        else:
            for r in range(d):
                qd_ref[r * L:(r + 1) * L, :] = qf_ref[pl.ds(r, L, stride=d), :].astype(jnp.bfloat16)
                kd_ref[W + r * L:W + (r + 1) * L, :] = kf_ref[pl.ds(r, L, stride=d), :].astype(jnp.bfloat16)
                vd_ref[W + r * L:W + (r + 1) * L, 0:LANES] = vf_ref[pl.ds(r, L, stride=d), :].astype(jnp.bfloat16)
        for h in range(2):
            slope = slopes_ref[2 * p + h] * float(d)
            full = jnp.where(valid, slope * dist_f, jnp.inf)
            a_ref[0, h * W:(h + 1) * W, :] = full
            a_ref[1, h * W:(h + 1) * W, :] = jnp.where(c_i < W, jnp.inf, full)

        dst_o = ro_ref.at[pi] if d == 1 else po_ref
        dst_l = rl_ref.at[pi] if d == 1 else pl_ref

        def one_block(u, no_prev, dst_o=dst_o, dst_l=dst_l):
            row0 = pl.multiple_of(u * W, W)
            qb = qd_ref[pl.ds(row0, W), :].astype(jnp.float32)
            kk = kd_ref[pl.ds(row0, 2 * W), :]
            vv = vd_ref[pl.ds(row0, 2 * W), :]
            q2 = jnp.concatenate([jnp.where(first, qb, 0.0), jnp.where(first, 0.0, qb)],
                                 axis=0).astype(jnp.bfloat16)
            s = lax.dot_general(q2, kk, (((1,), (1,)), ((), ())),
                                preferred_element_type=jnp.float32)
            s = s - a_ref[no_prev]
            m = jnp.max(s, axis=1, keepdims=True)
            pr = jnp.exp2(s - m)
            pv = jnp.dot(pr.astype(jnp.bfloat16), vv, preferred_element_type=jnp.float32)
            l = pv[:, LANES:]
            o = pv[:, :LANES] / l
            lse = m + jnp.log2(l)
            dst_o[pl.ds(row0, W), :] = jnp.where(first, o[:W], o[W:])
            dst_l[pl.ds(row0, W), :] = jnp.where(first, lse[:W], lse[W:])

        def group(g, carry, nbk=nbk):
            for i in range(DIL_UNROLL):
                u = g * DIL_UNROLL + i
                if DIL_UNROLL % nbk == 0:
                    no_prev = int(i % nbk == 0)
                elif i > 0:
                    no_prev = 0
                else:
                    no_prev = (lax.rem(u, nbk) == 0).astype(jnp.int32)
                one_block(u, no_prev)
            return carry

        assert DIL_UNROLL % nbk == 0 or nbk % DIL_UNROLL == 0
        lax.fori_loop(0, n_blocks // DIL_UNROLL, group, 0)

        if d > 1:
            for r in range(d):
                ro_ref[pi, pl.ds(r, L, stride=d), :] = po_ref[r * L:(r + 1) * L, :]
                rl_ref[pi, pl.ds(r, L, stride=d), :] = pl_ref[r * L:(r + 1) * L, :]

    def mix(c, carry):
        row0 = pl.multiple_of(c * W, W)
        ls = [rl_ref[i, pl.ds(row0, W), :] for i in range(3)]
        mx = jnp.maximum(jnp.maximum(ls[0], ls[1]), ls[2])
        es = [jnp.exp2(x - mx) for x in ls]
        den = es[0] + es[1] + es[2]
        num = (es[0] * ro_ref[0, pl.ds(row0, W), :] + es[1] * ro_ref[1, pl.ds(row0, W), :]
               + es[2] * ro_ref[2, pl.ds(row0, W), :])
        o_ref[pl.ds(row0, W), :] = (num / den).astype(o_ref.dtype)
        return carry

    lax.fori_loop(0, n_blocks, mix, 0)


def _dil_attn(proj, slopes, B, S):
    T = proj.shape[0]
    n_pairs = N_DIL_HEADS // 2
    base = 3 * N_DIFF_HEADS * HEAD_DIM // LANES
    W = DIL_W
    return pl.pallas_call(
        functools.partial(_dil_attn_kernel, S=S),
        grid=(B, n_pairs),
        in_specs=[
            pl.BlockSpec(memory_space=pltpu.SMEM),
            pl.BlockSpec((S, LANES), lambda b, p: (b, base + p)),
            pl.BlockSpec((S, LANES), lambda b, p: (b, base + n_pairs + p)),
            pl.BlockSpec((S, LANES), lambda b, p: (b, base + 2 * n_pairs + p)),
        ],
        out_specs=pl.BlockSpec((S, LANES), lambda b, p: (b, p)),
        out_shape=jax.ShapeDtypeStruct((T, N_DIL_HEADS * HEAD_DIM), jnp.bfloat16),
        scratch_shapes=[
            pltpu.VMEM((S, LANES), jnp.float32),
            pltpu.VMEM((S, LANES), jnp.float32),
            pltpu.VMEM((S, LANES), jnp.float32),
            pltpu.VMEM((S, LANES), jnp.bfloat16),
            pltpu.VMEM((S + W, LANES), jnp.bfloat16),
            pltpu.VMEM((S + W, 2 * LANES), jnp.bfloat16),
            pltpu.VMEM((2, 2 * W, 2 * W), jnp.float32),
            pltpu.VMEM((S, LANES), jnp.float32),
            pltpu.VMEM((S, LANES), jnp.float32),
            pltpu.VMEM((3, S, LANES), jnp.float32),
            pltpu.VMEM((3, S, LANES), jnp.float32),
        ],
        compiler_params=_cparams(("arbitrary", "arbitrary")),
        name="dil_attn",
    )(slopes, proj, proj, proj)


def _post_attn_kernel(x_ref, md_ref, ml_ref, wo_ref, nw_ref, rw_ref, rb_ref,
                      x1_ref, h2_ref, route_ref, gate_ref, cnt_ref, run_ref, tri_ref, *, tm):
    i = pl.program_id(0)

    @pl.when(i == 0)
    def _():
        run_ref[...] = jnp.zeros(run_ref.shape, jnp.float32)
        r_i = lax.broadcasted_iota(jnp.int32, (tm, tm), 0)
        c_i = lax.broadcasted_iota(jnp.int32, (tm, tm), 1)
        tri_ref[...] = jnp.where(c_i < r_i, 1.0, 0.0).astype(jnp.bfloat16)

    half = md_ref.shape[1]
    y = jnp.dot(md_ref[...], wo_ref[0:half, :], preferred_element_type=jnp.float32)
    y = y + jnp.dot(ml_ref[...], wo_ref[half:, :], preferred_element_type=jnp.float32)
    x1 = x_ref[...] + y
    x1_ref[...] = x1
    ms = jnp.mean(x1 * x1, axis=-1, keepdims=True)
    h2 = (x1 * lax.rsqrt(ms + NORM_EPS)) * nw_ref[...]
    for c in range(D_MODEL // LANES):
        h2_ref[pl.ds(c, tm, stride=SUBLANES), :] = h2[:, c * LANES:(c + 1) * LANES]
    hh = h2.astype(jnp.bfloat16)
    hl = (h2 - hh.astype(jnp.float32)).astype(jnp.bfloat16)
    lh = jnp.dot(hh, rw_ref[...], preferred_element_type=jnp.float32)
    ll = jnp.dot(hl, rw_ref[...], preferred_element_type=jnp.float32)
    logits = (lh[:, :LANES] + lh[:, LANES:]) + (ll[:, :LANES] + ll[:, LANES:]) + rb_ref[...]
    lane = lax.broadcasted_iota(jnp.int32, (tm, LANES), 1)
    work = jnp.where(lane < N_EXPERTS, logits, NEG_INF)
    vals, idxs, sels = [], [], []
    for _ in range(TOP_K):
        mk = jnp.max(work, axis=1, keepdims=True)
        ik = jnp.min(jnp.where(work == mk, lane, LANES), axis=1, keepdims=True)
        sel = lane == ik
        work = jnp.where(sel, NEG_INF, work)
        vals.append(mk)
        idxs.append(ik)
        sels.append(sel)
    es = [jnp.exp(v - vals[0]) for v in vals]
    den = es[0] + es[1] + es[2] + es[3]
    member = jnp.zeros((tm, LANES), jnp.float32)
    for sel in sels:
        member = jnp.where(sel, 1.0, member)
    before = jnp.dot(tri_ref[...], member.astype(jnp.bfloat16), preferred_element_type=jnp.float32)
    before = before + run_ref[...]
    route = jnp.zeros((tm, LANES), jnp.int32)
    gates = jnp.zeros((tm, LANES), jnp.float32)
    for k in range(TOP_K):
        rank = jnp.sum(jnp.where(sels[k], before, 0.0), axis=1, keepdims=True)
        route = jnp.where(lane == k, idxs[k], route)
        route = jnp.where(lane == TOP_K + k, rank.astype(jnp.int32), route)
        gates = jnp.where(lane == k, es[k] / den, gates)
    route_ref[...] = route
    gate_ref[...] = gates
    run_new = run_ref[...] + jnp.sum(member, axis=0, keepdims=True)
    run_ref[...] = run_new
    cnt_ref[...] = jnp.broadcast_to(run_new, cnt_ref.shape).astype(jnp.int32)


def _post_attn(x2d, mix_d, mix_l, wo_bf16, nw, rw_cat, rb, tm=512):
    T, D = x2d.shape
    half = mix_d.shape[1]
    return pl.pallas_call(
        functools.partial(_post_attn_kernel, tm=tm),
        grid=(T // tm,),
        in_specs=[
            pl.BlockSpec((tm, D), lambda i: (i, 0)),
            pl.BlockSpec((tm, half), lambda i: (i, 0)),
            pl.BlockSpec((tm, half), lambda i: (i, 0)),
            pl.BlockSpec((D, D), lambda i: (0, 0)),
            pl.BlockSpec((1, D), lambda i: (0, 0)),
            pl.BlockSpec((D, 2 * LANES), lambda i: (0, 0)),
            pl.BlockSpec((1, LANES), lambda i: (0, 0)),
        ],
        out_specs=[
            pl.BlockSpec((tm, D), lambda i: (i, 0)),
            pl.BlockSpec((tm * SUBLANES, LANES), lambda i: (i, 0)),
            pl.BlockSpec((tm, LANES), lambda i: (i, 0)),
            pl.BlockSpec((tm, LANES), lambda i: (i, 0)),
            pl.BlockSpec((8, LANES), lambda i: (0, 0)),
        ],
        out_shape=[
            jax.ShapeDtypeStruct((T, D), jnp.float32),
            jax.ShapeDtypeStruct((T * SUBLANES, LANES), jnp.float32),
            jax.ShapeDtypeStruct((T, LANES), jnp.int32),
            jax.ShapeDtypeStruct((T, LANES), jnp.float32),
            jax.ShapeDtypeStruct((8, LANES), jnp.int32),
        ],
        scratch_shapes=[pltpu.VMEM((1, LANES), jnp.float32), pltpu.VMEM((tm, tm), jnp.bfloat16)],
        compiler_params=_cparams(("arbitrary",)),
        name="post_attn",
    )(x2d, mix_d, mix_l, wo_bf16, nw, rw_cat, rb)


def _token_rows(ref, t, n_tokens=1):
    start = pl.multiple_of(t * SUBLANES, SUBLANES)
    return ref.at[pl.ds(start, n_tokens * SUBLANES), :]


def _dispatch_kernel(dest_ref, cnt_ref, pcnt_ref, pst_ref, h_ref, xs_ref, z_ref, sem, zsem, *, td):
    i = pl.program_id(0)
    n = pl.num_programs(0)

    def issue(t, carry):
        for k in range(TOP_K):
            pltpu.make_async_copy(_token_rows(h_ref, t),
                                  _token_rows(xs_ref, dest_ref[(i * td + t) * TOP_K + k]),
                                  sem).start(priority=k % 2)
        return carry

    lax.fori_loop(0, td, issue, 0)

    for k in range(TOP_K):
        pltpu.make_async_copy(h_ref, _token_rows(xs_ref, 0, td), sem).wait()

    @pl.when(i == n - 1)
    def _():
        z_ref[...] = jnp.zeros(z_ref.shape, z_ref.dtype)

        def pad_copy(row):
            return pltpu.make_async_copy(_token_rows(z_ref, 0), _token_rows(xs_ref, row), zsem)

        def per_expert(e, carry):
            base = pst_ref[e]

            def start(r, c):
                pad_copy(base + r).start()
                return c

            def wait(r, c):
                pad_copy(base + r).wait()
                return c

            lax.fori_loop(cnt_ref[e], pcnt_ref[e], start, 0)
            lax.fori_loop(cnt_ref[e], pcnt_ref[e], wait, 0)
            return carry

        lax.fori_loop(0, N_EXPERTS, per_expert, 0)

        def tail_copy(c):
            return pltpu.make_async_copy(z_ref, _token_rows(xs_ref, c * td, td), zsem)

        end = pst_ref[N_EXPERTS - 1] + pcnt_ref[N_EXPERTS - 1]
        first_chunk = end // td
        n_chunks = xs_ref.shape[0] // (td * SUBLANES)

        def tail_start(c, carry):
            tail_copy(c).start()
            return carry

        def tail_wait(c, carry):
            tail_copy(c).wait()
            return carry

        lax.fori_loop(first_chunk, n_chunks, tail_start, 0)
        lax.fori_loop(first_chunk, n_chunks, tail_wait, 0)


def _dispatch(dest_flat, counts, pcounts, pstarts, h2t, R, td=256):
    T = h2t.shape[0] // SUBLANES
    return pl.pallas_call(
        functools.partial(_dispatch_kernel, td=td),
        grid_spec=pltpu.PrefetchScalarGridSpec(
            num_scalar_prefetch=4,
            grid=(T // td,),
            in_specs=[pl.BlockSpec((td * SUBLANES, LANES), lambda i, *_: (i, 0))],
            out_specs=pl.BlockSpec(memory_space=pl.ANY),
            scratch_shapes=[
                pltpu.VMEM((td * SUBLANES, LANES), h2t.dtype),
                pltpu.SemaphoreType.DMA,
                pltpu.SemaphoreType.DMA,
            ],
        ),
        out_shape=jax.ShapeDtypeStruct((R * SUBLANES, LANES), h2t.dtype),
        compiler_params=_cparams(("arbitrary",)),
        name="dispatch",
    )(dest_flat, counts, pcounts, pstarts, h2t)


def _experts_kernel(be_ref, used_ref, xs_ref, wgu_ref, bgu_ref, wd_ref, bd_ref, o_ref,
                    wgu_bf, wd_bf, *, n_chunk):
    i = pl.program_id(0)
    prev = be_ref[jnp.maximum(i - 1, 0)]
    new_expert = jnp.logical_or(i == 0, be_ref[i] != prev)

    @pl.when(new_expert)
    def _():
        wgu_bf[...] = wgu_ref[0].astype(jnp.bfloat16)
        wd_bf[...] = wd_ref[0].astype(jnp.bfloat16)

    blk = xs_ref.shape[0] // SUBLANES
    n_lane_chunks = D_MODEL // LANES

    @pl.when(i < used_ref[0])
    def _():
        xb = jnp.concatenate(
            [xs_ref[pl.ds(c, blk, stride=SUBLANES), :].astype(jnp.bfloat16)
             for c in range(n_lane_chunks)], axis=1)
        acc = jnp.zeros((blk, D_MODEL), jnp.float32)
        for c in range(D_FF // n_chunk):
            gs = slice(c * n_chunk, (c + 1) * n_chunk)
            us = slice(D_FF + c * n_chunk, D_FF + (c + 1) * n_chunk)
            g = jnp.dot(xb, wgu_bf[:, gs], preferred_element_type=jnp.float32) + bgu_ref[0, :, gs]
            u = jnp.dot(xb, wgu_bf[:, us], preferred_element_type=jnp.float32) + bgu_ref[0, :, us]
            g = jnp.minimum(g, SWIGLU_LIMIT)
            u = jnp.clip(u, -SWIGLU_LIMIT, SWIGLU_LIMIT)
            act = (u + 1.0) * (g * jax.nn.sigmoid(SWIGLU_ALPHA * g))
            acc = acc + jnp.dot(act.astype(jnp.bfloat16), wd_bf[gs, :],
                                preferred_element_type=jnp.float32)
        res = acc + bd_ref[0]
        for c in range(n_lane_chunks):
            o_ref[pl.ds(c, blk, stride=SUBLANES), :] = res[:, c * LANES:(c + 1) * LANES]

    @pl.when(i >= used_ref[0])
    def _():
        o_ref[...] = jnp.zeros(o_ref.shape, o_ref.dtype)


def _experts(block_e, used, xs, w_gate_up, b_gate_up, w_down, b_down, blk=MOE_BLK):
    R = xs.shape[0] // SUBLANES
    D = D_MODEL
    n_blocks = R // blk
    F2 = w_gate_up.shape[2]

    def row_map(i, be, used):
        return (i, 0)

    return pl.pallas_call(
        functools.partial(_experts_kernel, n_chunk=512),
        grid_spec=pltpu.PrefetchScalarGridSpec(
            num_scalar_prefetch=2,
            grid=(n_blocks,),
            in_specs=[
                pl.BlockSpec((blk * SUBLANES, LANES), row_map),
                pl.BlockSpec((1, D, F2), lambda i, be, used: (be[i], 0, 0)),
                pl.BlockSpec((1, 1, F2), lambda i, be, used: (be[i], 0, 0)),
                pl.BlockSpec((1, D_FF, D), lambda i, be, used: (be[i], 0, 0)),
                pl.BlockSpec((1, 1, D), lambda i, be, used: (be[i], 0, 0)),
            ],
            out_specs=pl.BlockSpec((blk * SUBLANES, LANES), row_map),
            scratch_shapes=[
                pltpu.VMEM((D, F2), jnp.bfloat16),
                pltpu.VMEM((D_FF, D), jnp.bfloat16),
            ],
        ),
        out_shape=jax.ShapeDtypeStruct((R * SUBLANES, LANES), jnp.float32),
        compiler_params=_cparams(("arbitrary",)),
        name="experts",
    )(block_e, used, xs, w_gate_up, b_gate_up.reshape(N_EXPERTS, 1, F2),
      w_down, b_down.reshape(N_EXPERTS, 1, D))


def _combine_kernel(dest_ref, x1_ref, g_ref, nw_ref, rows_ref, o_ref, buf_ref, sems, *, tc):
    i = pl.program_id(0)
    n = pl.num_programs(0)
    slot = i & 1

    def issue_tile(tile, s):
        def issue(t, carry):
            for k in range(TOP_K):
                d = dest_ref[(tile * tc + t) * TOP_K + k]
                pltpu.make_async_copy(_token_rows(rows_ref, d), _token_rows(buf_ref.at[s, k], t),
                                      sems.at[s]).start(priority=k % 2)
            return carry

        lax.fori_loop(0, tc, issue, 0)

    @pl.when(i == 0)
    def _():
        issue_tile(0, 0)

    for s in range(2):
        @pl.when((i + 1 < n) & (slot == 1 - s))
        def _(s=s):
            issue_tile(i + 1, s)

    for k in range(TOP_K):
        pltpu.make_async_copy(_token_rows(rows_ref, 0, tc), buf_ref.at[slot, k],
                              sems.at[slot]).wait()

    g = g_ref[...]
    x1 = x1_ref[...]
    chunks = []
    for c in range(D_MODEL // LANES):
        yc = x1[:, c * LANES:(c + 1) * LANES]
        for k in range(TOP_K):
            yc = yc + g[:, k:k + 1] * buf_ref[slot, k, pl.ds(c, tc, stride=SUBLANES), :]
        chunks.append(yc)
    y = jnp.concatenate(chunks, axis=1)
    ms = jnp.mean(y * y, axis=-1, keepdims=True)
    o_ref[...] = (y * lax.rsqrt(ms + NORM_EPS)) * nw_ref[...]


def _combine(dest_flat, x1, gates, nw, rows, tc=256):
    T, D = x1.shape
    return pl.pallas_call(
        functools.partial(_combine_kernel, tc=tc),
        grid_spec=pltpu.PrefetchScalarGridSpec(
            num_scalar_prefetch=1,
            grid=(T // tc,),
            in_specs=[
                pl.BlockSpec((tc, D), lambda i, *_: (i, 0)),
                pl.BlockSpec((tc, LANES), lambda i, *_: (i, 0)),
                pl.BlockSpec((1, D), lambda i, *_: (0, 0)),
                pl.BlockSpec(memory_space=pl.ANY),
            ],
            out_specs=pl.BlockSpec((tc, D), lambda i, *_: (i, 0)),
            scratch_shapes=[
                pltpu.VMEM((2, TOP_K, tc * SUBLANES, LANES), jnp.float32),
                pltpu.SemaphoreType.DMA((2,)),
            ],
        ),
        out_shape=jax.ShapeDtypeStruct((T, D), jnp.float32),
        compiler_params=_cparams(("arbitrary",)),
        name="combine",
    )(dest_flat, x1, gates, nw, rows)


def kernel(x, attn_norm_w, w_in, diff_lambda_q1, diff_lambda_k1, diff_lambda_q2, diff_lambda_k2,
           diff_subln_w, w_out, ffn_norm_w, router_w, router_b, w_gate_up, b_gate_up, w_down,
           b_down, final_norm_w):
    B, S, D = x.shape
    T = B * S
    f32 = jnp.float32
    n = jnp.arange(1, N_HEADS_TOTAL + 1, dtype=f32)
    slopes = jnp.exp2(-8.0 * n / N_HEADS_TOTAL)
    diff_slopes = slopes[0::2] * LOG2E
    dil_slopes = slopes[1::2] * LOG2E
    depth = attn_norm_w.shape[0]
    assert depth == 1, "the combine kernel fuses the final norm, so exactly one layer is supported"
    x2d = x.reshape(T, D)
    diff_w = N_DIFF_HEADS * HEAD_DIM
    colscale = jnp.concatenate([
        jnp.full((diff_w,), DIFF_QK_DIM ** -0.5 * LOG2E, f32), jnp.ones((2 * diff_w,), f32),
        jnp.full((N_DIL_HEADS * HEAD_DIM,), HEAD_DIM ** -0.5 * LOG2E, f32),
        jnp.ones((2 * N_DIL_HEADS * HEAD_DIM,), f32)]).reshape(1, IN_COLS)
    n_blocks = -(-(T * TOP_K + N_EXPERTS * (MOE_BLK - 1)) // MOE_BLK)
    R = n_blocks * MOE_BLK

    for l in range(depth):
        proj = _in_proj(x2d, attn_norm_w[l].reshape(1, D), w_in[l].astype(jnp.bfloat16), colscale)
        lam_init = 0.8 - 0.6 * math.exp(-0.3 * l)
        lam = (jnp.exp(jnp.sum(diff_lambda_q1[l] * diff_lambda_k1[l]).astype(f32))
               - jnp.exp(jnp.sum(diff_lambda_q2[l] * diff_lambda_k2[l]).astype(f32))
               + lam_init).reshape(1)
        subln2 = jnp.tile(diff_subln_w[l], 2).reshape(1, LANES)
        mix_d = _diff_attn(proj, diff_slopes, lam, subln2, B, S, 1.0 - lam_init)
        mix_l = _dil_attn(proj, dil_slopes, B, S)

        rw = jnp.zeros((D, LANES), f32).at[:, :N_EXPERTS].set(router_w[l])
        rw_hi = rw.astype(jnp.bfloat16)
        rw_lo = (rw - rw_hi.astype(f32)).astype(jnp.bfloat16)
        rb = jnp.zeros((1, LANES), f32).at[0, :N_EXPERTS].set(router_b[l])
        x1, h2, route, gates, cnt = _post_attn(
            x2d, mix_d, mix_l, w_out[l].astype(jnp.bfloat16), ffn_norm_w[l].reshape(1, D),
            jnp.concatenate([rw_hi, rw_lo], axis=1), rb)

        counts = cnt[0, :N_EXPERTS]
        pcounts = (counts + MOE_BLK - 1) // MOE_BLK * MOE_BLK
        pends = jnp.cumsum(pcounts)
        pstarts = pends - pcounts
        top_i = route[:, :TOP_K]
        rank = route[:, TOP_K:2 * TOP_K]
        onehot = top_i[..., None] == jnp.arange(N_EXPERTS, dtype=jnp.int32)
        dest = jnp.sum(jnp.where(onehot, pstarts.astype(jnp.int32), 0), axis=-1) + rank
        dest_flat = dest.reshape(T * TOP_K).astype(jnp.int32)
        used = (pends[-1] // MOE_BLK).astype(jnp.int32)
        blk_start = jnp.arange(n_blocks, dtype=jnp.int32) * MOE_BLK
        be = jnp.minimum(jnp.sum(pends[None, :] <= blk_start[:, None], axis=1), N_EXPERTS - 1)
        be_last = jnp.max(jnp.where(jnp.arange(n_blocks) < used, be, 0))
        block_e = jnp.where(jnp.arange(n_blocks) < used, be, be_last).astype(jnp.int32)

        xs = _dispatch(dest_flat, counts.astype(jnp.int32), pcounts.astype(jnp.int32),
                       pstarts.astype(jnp.int32), h2, R)
        rows = _experts(block_e, used.reshape(1), xs, w_gate_up[l], b_gate_up[l], w_down[l], b_down[l])
        x2d = _combine(dest_flat, x1, gates, final_norm_w.reshape(1, D), rows)
    return x2d.reshape(B, S, D)
```

```python
import functools
import math

import jax
import jax.numpy as jnp
from jax import lax
from jax.experimental import pallas as pl
from jax.experimental.pallas import tpu as pltpu

D_MODEL = 1024
HEAD_DIM = 64
N_DIFF_HEADS = 8
DIFF_QK_DIM = 32
N_DIL_HEADS = 8
N_HEADS_TOTAL = 16
DIL_PATTERNS = ((128, 1), (512, 4), (2048, 16))
DIL_W = 128
N_EXPERTS = 32
TOP_K = 4
D_FF = D_MODEL
SWIGLU_LIMIT = 7.0
SWIGLU_ALPHA = 1.702
NORM_EPS = 1e-5
IN_COLS = 3072
LANES = 128
SUBLANES = 8
DIL_UNROLL = 8
LOG2E = math.log2(math.e)
MOE_BLK = 512
VMEM_LIMIT = 56 * 1024 * 1024

NEG_INF = float("-inf")


def _cparams(sem, vmem=VMEM_LIMIT):
    return pltpu.CompilerParams(dimension_semantics=sem, vmem_limit_bytes=vmem)


def _in_proj_kernel(x_ref, nw_ref, w_ref, cs_ref, o_ref, *, n_chunk):
    xf = x_ref[...]
    ms = jnp.mean(xf * xf, axis=-1, keepdims=True)
    h = (xf * lax.rsqrt(ms + NORM_EPS)) * nw_ref[...]
    hb = h.astype(jnp.bfloat16)
    n_cols = o_ref.shape[1]
    for c in range(n_cols // n_chunk):
        sl = slice(c * n_chunk, (c + 1) * n_chunk)
        r = jnp.dot(hb, w_ref[:, sl], preferred_element_type=jnp.float32)
        o_ref[:, sl] = (r * cs_ref[:, sl]).astype(o_ref.dtype)


def _in_proj(x2d, nw, w_bf16, colscale, tm=512):
    T, D = x2d.shape
    N = w_bf16.shape[1]
    return pl.pallas_call(
        functools.partial(_in_proj_kernel, n_chunk=512),
        grid=(T // tm,),
        in_specs=[
            pl.BlockSpec((tm, D), lambda i: (i, 0)),
            pl.BlockSpec((1, D), lambda i: (0, 0)),
            pl.BlockSpec((D, N), lambda i: (0, 0)),
            pl.BlockSpec((1, N), lambda i: (0, 0)),
        ],
        out_specs=pl.BlockSpec((tm, N), lambda i: (i, 0)),
        out_shape=jax.ShapeDtypeStruct((T, N), jnp.bfloat16),
        compiler_params=_cparams(("arbitrary",)),
        name="in_proj",
    )(x2d, nw, w_bf16, colscale)


def _diff_attn_kernel(slopes_ref, lam_ref, q_ref, k_ref, v_ref, w_ref, o_ref,
                      q4t_ref, pos_ref, vt_ref, mask_ref, s_ref, mt_ref, acc_ref,
                      *, tq, tk, post_scale):
    assert tq == tk
    p = pl.program_id(1)
    qi = pl.program_id(2)
    n_lanes = 4 * tq
    n_kv = v_ref.shape[0] // tk
    vt_rows = vt_ref.shape[2]

    lane = lax.broadcasted_iota(jnp.int32, (1, n_lanes), 1)
    slope_l = jnp.where(lane < 2 * tq, slopes_ref[2 * p], slopes_ref[2 * p + 1])

    @pl.when(qi == 0)
    def _():
        ones = jnp.ones((vt_rows - HEAD_DIM, tk), jnp.bfloat16)
        for jj in range(n_kv):
            vt = v_ref[jj * tk:(jj + 1) * tk, :].astype(jnp.float32).T.astype(jnp.bfloat16)
            for h in range(2):
                vt_ref[jj, h, 0:HEAD_DIM, :] = vt[h * HEAD_DIM:(h + 1) * HEAD_DIM, :]
                vt_ref[jj, h, HEAD_DIM:vt_rows, :] = ones
        hi = slope_l.astype(jnp.bfloat16).astype(jnp.float32)
        mid = (slope_l - hi).astype(jnp.bfloat16).astype(jnp.float32)
        lo = slope_l - hi - mid
        r16 = lax.broadcasted_iota(jnp.int32, (16, n_lanes), 0)
        rows = jnp.where(r16 == 0, hi, jnp.where(r16 == 1, mid, jnp.where(r16 == 2, lo, 0.0)))
        q4t_ref[LANES:LANES + 16, :] = rows.astype(jnp.bfloat16)
        q4t_ref[LANES + 16:, :] = jnp.zeros((LANES - 16, n_lanes), jnp.bfloat16)
        key_c = lax.broadcasted_iota(jnp.int32, (tk, LANES), 0)
        col_c = lax.broadcasted_iota(jnp.int32, (tk, LANES), 1)
        pos_ref[...] = jnp.where(col_c < 3, key_c, 0).astype(jnp.float32).astype(jnp.bfloat16)
        key_i = lax.broadcasted_iota(jnp.int32, (tk, n_lanes), 0)
        qry_i = lax.broadcasted_iota(jnp.int32, (tk, n_lanes), 1) & (tq - 1)
        mask_ref[...] = jnp.where(key_i <= qry_i, 0.0, NEG_INF)

    qt = q_ref[...].astype(jnp.float32).T
    sub = lax.broadcasted_iota(jnp.int32, (LANES, tq), 0) >> 5
    for c in range(4):
        q4t_ref[0:LANES, c * tq:(c + 1) * tq] = jnp.where(sub == c, qt, 0.0).astype(jnp.bfloat16)
    acc_ref[...] = jnp.zeros(acc_ref.shape, jnp.float32)

    def tile_off(n):
        return slope_l * jnp.asarray(n * tk).astype(jnp.float32)

    def scores(n, slot, mask):
        ka = jnp.concatenate([k_ref[n * tk:(n + 1) * tk, :], pos_ref[...]], axis=1)
        s = jnp.dot(ka, q4t_ref[...], preferred_element_type=jnp.float32)
        if mask == "diag":
            s = s + mask_ref[...]
        s_ref[slot] = s
        mt_ref[slot] = jnp.max(s, axis=0, keepdims=True) + tile_off(n)

    def weights(n, slot, m_prev):
        m_new = jnp.maximum(m_prev, mt_ref[slot])
        alpha = jnp.exp2(m_prev - m_new)
        pb = jnp.exp2(s_ref[slot] - (m_new - tile_off(n))).astype(jnp.bfloat16)
        for h in range(2):
            cols = slice(h * 2 * tq, (h + 1) * 2 * tq)
            pv = jnp.dot(vt_ref[n, h], pb[:, cols], preferred_element_type=jnp.float32)
            acc_ref[h] = alpha[:, cols] * acc_ref[h] + pv
        return m_new

    def run_tiles(last):
        scores(0, 0, "diag" if last == 0 else None)
        m = jnp.full((1, n_lanes), NEG_INF, jnp.float32)
        for t in range(last + 1):
            if t < last:
                scores(t + 1, (t + 1) & 1, "diag" if t + 1 == last else None)
            m = weights(t, t & 1, m)

    for c in range(n_kv):
        pl.when(qi == c)(functools.partial(run_tiles, c))

    lam = lam_ref[0]
    ys = []
    for h in range(2):
        acc = acc_ref[h]
        num = acc[0:HEAD_DIM]
        l = acc[HEAD_DIM:HEAD_DIM + 1]
        o = num[:, :tq] / l[:, :tq] - lam * (num[:, tq:] / l[:, tq:])
        ms = jnp.mean(o * o, axis=0, keepdims=True)
        ys.append(o * lax.rsqrt(ms + NORM_EPS))
    y = jnp.concatenate(ys, axis=0).T
    o_ref[...] = ((y * w_ref[...]) * post_scale).astype(o_ref.dtype)


def _diff_attn(proj, slopes, lam, subln_w2, B, S, post_scale, tq=256, tk=256):
    T = proj.shape[0]
    nq = S // tq
    n_pairs = N_DIFF_HEADS // 2
    kern = functools.partial(_diff_attn_kernel, tq=tq, tk=tk, post_scale=post_scale)
    return pl.pallas_call(
        kern,
        grid=(B, n_pairs, nq),
        in_specs=[
            pl.BlockSpec(memory_space=pltpu.SMEM),
            pl.BlockSpec(memory_space=pltpu.SMEM),
            pl.BlockSpec((tq, LANES), lambda b, p, i: (b * nq + i, p)),
            pl.BlockSpec((S, LANES), lambda b, p, i: (b, n_pairs + p)),
            pl.BlockSpec((S, LANES), lambda b, p, i: (b, 2 * n_pairs + p)),
            pl.BlockSpec((1, LANES), lambda b, p, i: (0, 0)),
        ],
        out_specs=pl.BlockSpec((tq, LANES), lambda b, p, i: (b * nq + i, p)),
        out_shape=jax.ShapeDtypeStruct((T, N_DIFF_HEADS * HEAD_DIM), jnp.bfloat16),
        scratch_shapes=[
            pltpu.VMEM((2 * LANES, 4 * tq), jnp.bfloat16),
            pltpu.VMEM((tk, LANES), jnp.bfloat16),
            pltpu.VMEM((S // tk, 2, HEAD_DIM + 16, tk), jnp.bfloat16),
            pltpu.VMEM((tk, 4 * tq), jnp.float32),
            pltpu.VMEM((2, tk, 4 * tq), jnp.float32),
            pltpu.VMEM((2, 1, 4 * tq), jnp.float32),
            pltpu.VMEM((2, HEAD_DIM + 16, 2 * tq), jnp.float32),
        ],
        compiler_params=_cparams(("arbitrary", "arbitrary", "arbitrary")),
        name="diff_attn",
    )(slopes, lam, proj, proj, proj, subln_w2)


def _dil_attn_kernel(slopes_ref, q_ref, k_ref, v_ref, o_ref,
                     qf_ref, kf_ref, vf_ref, qd_ref, kd_ref, vd_ref, a_ref,
                     po_ref, pl_ref, ro_ref, rl_ref, *, S):
    p = pl.program_id(1)
    W = DIL_W
    n_blocks = S // W
    qf_ref[...] = q_ref[...].astype(jnp.float32)
    kf_ref[...] = k_ref[...].astype(jnp.float32)
    vf_ref[...] = v_ref[...].astype(jnp.float32)
    @pl.when((pl.program_id(0) == 0) & (p == 0))
    def _():
        for i in range(len(DIL_PATTERNS)):
            kd_ref[i, 0:W, :] = jnp.zeros((W, LANES), jnp.bfloat16)
            vd_ref[i, 0:W, 0:LANES] = jnp.zeros((W, LANES), jnp.bfloat16)
            vd_ref[i, :, LANES:] = jnp.ones((S + W, LANES), jnp.bfloat16)

    lane = lax.broadcasted_iota(jnp.int32, (W, LANES), 1)
    first = lane < HEAD_DIM
    r_i = lax.broadcasted_iota(jnp.int32, (W, 2 * W), 0)
    c_i = lax.broadcasted_iota(jnp.int32, (W, 2 * W), 1)
    dist = W + r_i - c_i
    valid = (dist >= 0) & (dist <= W)
    dist_f = dist.astype(jnp.float32)

    for pi, (_, d) in enumerate(DIL_PATTERNS):
        L = S // d
        nbk = L // W
        qd, kd, vd, bias = qd_ref.at[pi], kd_ref.at[pi], vd_ref.at[pi], a_ref.at[pi]
        po, plse = po_ref.at[pi], pl_ref.at[pi]
        if d == 1:
            qd[...] = q_ref[...]
            kd[W:W + S, :] = k_ref[...]
            vd[W:W + S, 0:LANES] = v_ref[...]
        else:
            for r in range(d):
                qd[r * L:(r + 1) * L, :] = qf_ref[pl.ds(r, L, stride=d), :].astype(jnp.bfloat16)
                kd[W + r * L:W + (r + 1) * L, :] = kf_ref[pl.ds(r, L, stride=d), :].astype(jnp.bfloat16)
                vd[W + r * L:W + (r + 1) * L, 0:LANES] = vf_ref[pl.ds(r, L, stride=d), :].astype(jnp.bfloat16)
        for h in range(2):
            slope = slopes_ref[2 * p + h] * float(d)
            full = jnp.where(valid, slope * dist_f, jnp.inf)
            bias[0, h * W:(h + 1) * W, :] = full
            bias[1, h * W:(h + 1) * W, :] = jnp.where(c_i < W, jnp.inf, full)

        dst_o = ro_ref.at[pi] if d == 1 else po
        dst_l = rl_ref.at[pi] if d == 1 else plse

        def one_block(u, no_prev, dst_o=dst_o, dst_l=dst_l, qd=qd, kd=kd, vd=vd, bias=bias):
            row0 = u * W
            qb = qd[pl.ds(row0, W), :].astype(jnp.float32)
            kk = kd[pl.ds(row0, 2 * W), :]
            vv = vd[pl.ds(row0, 2 * W), :]
            q2 = jnp.concatenate([jnp.where(first, qb, 0.0), jnp.where(first, 0.0, qb)],
                                 axis=0).astype(jnp.bfloat16)
            s = lax.dot_general(q2, kk, (((1,), (1,)), ((), ())),
                                preferred_element_type=jnp.float32)
            s = s - bias[no_prev]
            m = jnp.max(s, axis=1, keepdims=True)
            pr = jnp.exp2(s - m)
            pv = jnp.dot(pr.astype(jnp.bfloat16), vv, preferred_element_type=jnp.float32)
            l = pv[:, LANES:]
            o = pv[:, :LANES] / l
            lse = m + jnp.log2(l)
            dst_o[pl.ds(row0, W), :] = jnp.where(first, o[:W], o[W:])
            dst_l[pl.ds(row0, W), :] = jnp.where(first, lse[:W], lse[W:])

        for u in range(n_blocks):
            one_block(u, int(u % nbk == 0))

        if d > 1:
            for r in range(d):
                ro_ref[pi, pl.ds(r, L, stride=d), :] = po[r * L:(r + 1) * L, :]
                rl_ref[pi, pl.ds(r, L, stride=d), :] = plse[r * L:(r + 1) * L, :]

    def mix(c, carry):
        row0 = pl.multiple_of(c * W, W)
        ls = [rl_ref[i, pl.ds(row0, W), :] for i in range(3)]
        mx = jnp.maximum(jnp.maximum(ls[0], ls[1]), ls[2])
        es = [jnp.exp2(x - mx) for x in ls]
        den = es[0] + es[1] + es[2]
        num = (es[0] * ro_ref[0, pl.ds(row0, W), :] + es[1] * ro_ref[1, pl.ds(row0, W), :]
               + es[2] * ro_ref[2, pl.ds(row0, W), :])
        o_ref[pl.ds(row0, W), :] = (num / den).astype(o_ref.dtype)
        return carry

    lax.fori_loop(0, n_blocks, mix, 0)


def _dil_attn(proj, slopes, B, S):
    T = proj.shape[0]
    n_pairs = N_DIL_HEADS // 2
    base = 3 * N_DIFF_HEADS * HEAD_DIM // LANES
    W = DIL_W
    return pl.pallas_call(
        functools.partial(_dil_attn_kernel, S=S),
        grid=(B, n_pairs),
        in_specs=[
            pl.BlockSpec(memory_space=pltpu.SMEM),
            pl.BlockSpec((S, LANES), lambda b, p: (b, base + p)),
            pl.BlockSpec((S, LANES), lambda b, p: (b, base + n_pairs + p)),
            pl.BlockSpec((S, LANES), lambda b, p: (b, base + 2 * n_pairs + p)),
        ],
        out_specs=pl.BlockSpec((S, LANES), lambda b, p: (b, p)),
        out_shape=jax.ShapeDtypeStruct((T, N_DIL_HEADS * HEAD_DIM), jnp.bfloat16),
        scratch_shapes=[
            pltpu.VMEM((S, LANES), jnp.float32),
            pltpu.VMEM((S, LANES), jnp.float32),
            pltpu.VMEM((S, LANES), jnp.float32),
            pltpu.VMEM((3, S, LANES), jnp.bfloat16),
            pltpu.VMEM((3, S + W, LANES), jnp.bfloat16),
            pltpu.VMEM((3, S + W, 2 * LANES), jnp.bfloat16),
            pltpu.VMEM((3, 2, 2 * W, 2 * W), jnp.float32),
            pltpu.VMEM((3, S, LANES), jnp.float32),
            pltpu.VMEM((3, S, LANES), jnp.float32),
            pltpu.VMEM((3, S, LANES), jnp.float32),
            pltpu.VMEM((3, S, LANES), jnp.float32),
        ],
        compiler_params=_cparams(("arbitrary", "arbitrary")),
        name="dil_attn",
    )(slopes, proj, proj, proj)


def _post_attn_kernel(x_ref, md_ref, ml_ref, wo_ref, nw_ref, rw_ref, rb_ref,
                      x1_ref, h2_ref, route_ref, gate_ref, cnt_ref, run_ref, tri_ref, *, tm):
    i = pl.program_id(0)

    @pl.when(i == 0)
    def _():
        run_ref[...] = jnp.zeros(run_ref.shape, jnp.float32)
        r_i = lax.broadcasted_iota(jnp.int32, (tm, tm), 0)
        c_i = lax.broadcasted_iota(jnp.int32, (tm, tm), 1)
        tri_ref[...] = jnp.where(c_i < r_i, 1.0, 0.0).astype(jnp.bfloat16)

    half = md_ref.shape[1]
    y = jnp.dot(md_ref[...], wo_ref[0:half, :], preferred_element_type=jnp.float32)
    y = y + jnp.dot(ml_ref[...], wo_ref[half:, :], preferred_element_type=jnp.float32)
    x1 = x_ref[...] + y
    x1_ref[...] = x1
    ms = jnp.mean(x1 * x1, axis=-1, keepdims=True)
    h2 = (x1 * lax.rsqrt(ms + NORM_EPS)) * nw_ref[...]
    for c in range(D_MODEL // LANES):
        h2_ref[pl.ds(c, tm, stride=SUBLANES), :] = h2[:, c * LANES:(c + 1) * LANES]
    hh = h2.astype(jnp.bfloat16)
    hl = (h2 - hh.astype(jnp.float32)).astype(jnp.bfloat16)
    lh = jnp.dot(hh, rw_ref[...], preferred_element_type=jnp.float32)
    ll = jnp.dot(hl, rw_ref[...], preferred_element_type=jnp.float32)
    logits = (lh[:, :LANES] + lh[:, LANES:]) + (ll[:, :LANES] + ll[:, LANES:]) + rb_ref[...]
    lane = lax.broadcasted_iota(jnp.int32, (tm, LANES), 1)
    work = jnp.where(lane < N_EXPERTS, logits, NEG_INF)
    vals, idxs, sels = [], [], []
    for _ in range(TOP_K):
        mk = jnp.max(work, axis=1, keepdims=True)
        ik = jnp.min(jnp.where(work == mk, lane, LANES), axis=1, keepdims=True)
        sel = lane == ik
        work = jnp.where(sel, NEG_INF, work)
        vals.append(mk)
        idxs.append(ik)
        sels.append(sel)
    es = [jnp.exp(v - vals[0]) for v in vals]
    den = es[0] + es[1] + es[2] + es[3]
    member = jnp.zeros((tm, LANES), jnp.float32)
    for sel in sels:
        member = jnp.where(sel, 1.0, member)
    before = jnp.dot(tri_ref[...], member.astype(jnp.bfloat16), preferred_element_type=jnp.float32)
    before = before + run_ref[...]
    route = jnp.zeros((tm, LANES), jnp.int32)
    gates = jnp.zeros((tm, LANES), jnp.float32)
    for k in range(TOP_K):
        rank = jnp.sum(jnp.where(sels[k], before, 0.0), axis=1, keepdims=True)
        route = jnp.where(lane == k, idxs[k], route)
        route = jnp.where(lane == TOP_K + k, rank.astype(jnp.int32), route)
        gates = jnp.where(lane == k, es[k] / den, gates)
    route_ref[...] = route
    gate_ref[...] = gates
    run_new = run_ref[...] + jnp.sum(member, axis=0, keepdims=True)
    run_ref[...] = run_new
    cnt_ref[...] = jnp.broadcast_to(run_new, cnt_ref.shape).astype(jnp.int32)


def _post_attn(x2d, mix_d, mix_l, wo_bf16, nw, rw_cat, rb, tm=512):
    T, D = x2d.shape
    half = mix_d.shape[1]
    return pl.pallas_call(
        functools.partial(_post_attn_kernel, tm=tm),
        grid=(T // tm,),
        in_specs=[
            pl.BlockSpec((tm, D), lambda i: (i, 0)),
            pl.BlockSpec((tm, half), lambda i: (i, 0)),
            pl.BlockSpec((tm, half), lambda i: (i, 0)),
            pl.BlockSpec((D, D), lambda i: (0, 0)),
            pl.BlockSpec((1, D), lambda i: (0, 0)),
            pl.BlockSpec((D, 2 * LANES), lambda i: (0, 0)),
            pl.BlockSpec((1, LANES), lambda i: (0, 0)),
        ],
        out_specs=[
            pl.BlockSpec((tm, D), lambda i: (i, 0)),
            pl.BlockSpec((tm * SUBLANES, LANES), lambda i: (i, 0)),
            pl.BlockSpec((tm, LANES), lambda i: (i, 0)),
            pl.BlockSpec((tm, LANES), lambda i: (i, 0)),
            pl.BlockSpec((8, LANES), lambda i: (0, 0)),
        ],
        out_shape=[
            jax.ShapeDtypeStruct((T, D), jnp.float32),
            jax.ShapeDtypeStruct((T * SUBLANES, LANES), jnp.float32),
            jax.ShapeDtypeStruct((T, LANES), jnp.int32),
            jax.ShapeDtypeStruct((T, LANES), jnp.float32),
            jax.ShapeDtypeStruct((8, LANES), jnp.int32),
        ],
        scratch_shapes=[pltpu.VMEM((1, LANES), jnp.float32), pltpu.VMEM((tm, tm), jnp.bfloat16)],
        compiler_params=_cparams(("arbitrary",)),
        name="post_attn",
    )(x2d, mix_d, mix_l, wo_bf16, nw, rw_cat, rb)


def _token_rows(ref, t, n_tokens=1):
    start = pl.multiple_of(t * SUBLANES, SUBLANES)
    return ref.at[pl.ds(start, n_tokens * SUBLANES), :]


def _dispatch_kernel(dest_ref, cnt_ref, pcnt_ref, pst_ref, h_ref, xs_ref, z_ref, sem, zsem, *, td):
    i = pl.program_id(0)
    n = pl.num_programs(0)

    def issue(t, carry):
        for k in range(TOP_K):
            pltpu.make_async_copy(_token_rows(h_ref, t),
                                  _token_rows(xs_ref, dest_ref[(i * td + t) * TOP_K + k]),
                                  sem).start(priority=k % 2)
        return carry

    lax.fori_loop(0, td, issue, 0)

    for k in range(TOP_K):
        pltpu.make_async_copy(h_ref, _token_rows(xs_ref, 0, td), sem).wait()

    @pl.when(i == n - 1)
    def _():
        z_ref[...] = jnp.zeros(z_ref.shape, z_ref.dtype)

        def pad_copy(row):
            return pltpu.make_async_copy(_token_rows(z_ref, 0), _token_rows(xs_ref, row), zsem)

        def per_expert(e, carry):
            base = pst_ref[e]

            def start(r, c):
                pad_copy(base + r).start()
                return c

            def wait(r, c):
                pad_copy(base + r).wait()
                return c

            lax.fori_loop(cnt_ref[e], pcnt_ref[e], start, 0)
            lax.fori_loop(cnt_ref[e], pcnt_ref[e], wait, 0)
            return carry

        lax.fori_loop(0, N_EXPERTS, per_expert, 0)

        def tail_copy(c):
            return pltpu.make_async_copy(z_ref, _token_rows(xs_ref, c * td, td), zsem)

        end = pst_ref[N_EXPERTS - 1] + pcnt_ref[N_EXPERTS - 1]
        first_chunk = end // td
        n_chunks = xs_ref.shape[0] // (td * SUBLANES)

        def tail_start(c, carry):
            tail_copy(c).start()
            return carry

        def tail_wait(c, carry):
            tail_copy(c).wait()
            return carry

        lax.fori_loop(first_chunk, n_chunks, tail_start, 0)
        lax.fori_loop(first_chunk, n_chunks, tail_wait, 0)


def _dispatch(dest_flat, counts, pcounts, pstarts, h2t, R, td=256):
    T = h2t.shape[0] // SUBLANES
    return pl.pallas_call(
        functools.partial(_dispatch_kernel, td=td),
        grid_spec=pltpu.PrefetchScalarGridSpec(
            num_scalar_prefetch=4,
            grid=(T // td,),
            in_specs=[pl.BlockSpec((td * SUBLANES, LANES), lambda i, *_: (i, 0))],
            out_specs=pl.BlockSpec(memory_space=pl.ANY),
            scratch_shapes=[
                pltpu.VMEM((td * SUBLANES, LANES), h2t.dtype),
                pltpu.SemaphoreType.DMA,
                pltpu.SemaphoreType.DMA,
            ],
        ),
        out_shape=jax.ShapeDtypeStruct((R * SUBLANES, LANES), h2t.dtype),
        compiler_params=_cparams(("arbitrary",)),
        name="dispatch",
    )(dest_flat, counts, pcounts, pstarts, h2t)


def _experts_kernel(be_ref, used_ref, xs_ref, wgu_ref, bgu_ref, wd_ref, bd_ref, o_ref,
                    wgu_bf, wd_bf, *, n_chunk):
    i = pl.program_id(0)
    prev = be_ref[jnp.maximum(i - 1, 0)]
    new_expert = jnp.logical_or(i == 0, be_ref[i] != prev)

    @pl.when(new_expert)
    def _():
        wgu_bf[...] = wgu_ref[0].astype(jnp.bfloat16)
        wd_bf[...] = wd_ref[0].astype(jnp.bfloat16)

    blk = xs_ref.shape[0] // SUBLANES
    n_lane_chunks = D_MODEL // LANES

    @pl.when(i < used_ref[0])
    def _():
        xb = jnp.concatenate(
            [xs_ref[pl.ds(c, blk, stride=SUBLANES), :].astype(jnp.bfloat16)
             for c in range(n_lane_chunks)], axis=1)
        acc = jnp.zeros((blk, D_MODEL), jnp.float32)
        for c in range(D_FF // n_chunk):
            gs = slice(c * n_chunk, (c + 1) * n_chunk)
            us = slice(D_FF + c * n_chunk, D_FF + (c + 1) * n_chunk)
            g = jnp.dot(xb, wgu_bf[:, gs], preferred_element_type=jnp.float32) + bgu_ref[0, :, gs]
            u = jnp.dot(xb, wgu_bf[:, us], preferred_element_type=jnp.float32) + bgu_ref[0, :, us]
            g = jnp.minimum(g, SWIGLU_LIMIT)
            u = jnp.clip(u, -SWIGLU_LIMIT, SWIGLU_LIMIT)
            act = (u + 1.0) * (g * jax.nn.sigmoid(SWIGLU_ALPHA * g))
            acc = acc + jnp.dot(act.astype(jnp.bfloat16), wd_bf[gs, :],
                                preferred_element_type=jnp.float32)
        res = acc + bd_ref[0]
        for c in range(n_lane_chunks):
            o_ref[pl.ds(c, blk, stride=SUBLANES), :] = res[:, c * LANES:(c + 1) * LANES]

    @pl.when(i >= used_ref[0])
    def _():
        o_ref[...] = jnp.zeros(o_ref.shape, o_ref.dtype)


def _experts(block_e, used, xs, w_gate_up, b_gate_up, w_down, b_down, blk=MOE_BLK):
    R = xs.shape[0] // SUBLANES
    D = D_MODEL
    n_blocks = R // blk
    F2 = w_gate_up.shape[2]

    def row_map(i, be, used):
        return (i, 0)

    return pl.pallas_call(
        functools.partial(_experts_kernel, n_chunk=512),
        grid_spec=pltpu.PrefetchScalarGridSpec(
            num_scalar_prefetch=2,
            grid=(n_blocks,),
            in_specs=[
                pl.BlockSpec((blk * SUBLANES, LANES), row_map),
                pl.BlockSpec((1, D, F2), lambda i, be, used: (be[i], 0, 0)),
                pl.BlockSpec((1, 1, F2), lambda i, be, used: (be[i], 0, 0)),
                pl.BlockSpec((1, D_FF, D), lambda i, be, used: (be[i], 0, 0)),
                pl.BlockSpec((1, 1, D), lambda i, be, used: (be[i], 0, 0)),
            ],
            out_specs=pl.BlockSpec((blk * SUBLANES, LANES), row_map),
            scratch_shapes=[
                pltpu.VMEM((D, F2), jnp.bfloat16),
                pltpu.VMEM((D_FF, D), jnp.bfloat16),
            ],
        ),
        out_shape=jax.ShapeDtypeStruct((R * SUBLANES, LANES), jnp.float32),
        compiler_params=_cparams(("arbitrary",)),
        name="experts",
    )(block_e, used, xs, w_gate_up, b_gate_up.reshape(N_EXPERTS, 1, F2),
      w_down, b_down.reshape(N_EXPERTS, 1, D))


def _combine_kernel(dest_ref, x1_ref, g_ref, nw_ref, rows_ref, o_ref, buf_ref, sems, *, tc):
    i = pl.program_id(0)
    n = pl.num_programs(0)
    slot = i & 1

    def issue_tile(tile, s):
        def issue(t, carry):
            for k in range(TOP_K):
                d = dest_ref[(tile * tc + t) * TOP_K + k]
                pltpu.make_async_copy(_token_rows(rows_ref, d), _token_rows(buf_ref.at[s, k], t),
                                      sems.at[s]).start(priority=k % 2)
            return carry

        lax.fori_loop(0, tc, issue, 0)

    @pl.when(i == 0)
    def _():
        issue_tile(0, 0)

    for s in range(2):
        @pl.when((i + 1 < n) & (slot == 1 - s))
        def _(s=s):
            issue_tile(i + 1, s)

    for k in range(TOP_K):
        pltpu.make_async_copy(_token_rows(rows_ref, 0, tc), buf_ref.at[slot, k],
                              sems.at[slot]).wait()

    g = g_ref[...]
    x1 = x1_ref[...]
    chunks = []
    for c in range(D_MODEL // LANES):
        yc = x1[:, c * LANES:(c + 1) * LANES]
        for k in range(TOP_K):
            yc = yc + g[:, k:k + 1] * buf_ref[slot, k, pl.ds(c, tc, stride=SUBLANES), :]
        chunks.append(yc)
    y = jnp.concatenate(chunks, axis=1)
    ms = jnp.mean(y * y, axis=-1, keepdims=True)
    o_ref[...] = (y * lax.rsqrt(ms + NORM_EPS)) * nw_ref[...]


def _combine(dest_flat, x1, gates, nw, rows, tc=256):
    T, D = x1.shape
    return pl.pallas_call(
        functools.partial(_combine_kernel, tc=tc),
        grid_spec=pltpu.PrefetchScalarGridSpec(
            num_scalar_prefetch=1,
            grid=(T // tc,),
            in_specs=[
                pl.BlockSpec((tc, D), lambda i, *_: (i, 0)),
                pl.BlockSpec((tc, LANES), lambda i, *_: (i, 0)),
                pl.BlockSpec((1, D), lambda i, *_: (0, 0)),
                pl.BlockSpec(memory_space=pl.ANY),
            ],
            out_specs=pl.BlockSpec((tc, D), lambda i, *_: (i, 0)),
            scratch_shapes=[
                pltpu.VMEM((2, TOP_K, tc * SUBLANES, LANES), jnp.float32),
                pltpu.SemaphoreType.DMA((2,)),
            ],
        ),
        out_shape=jax.ShapeDtypeStruct((T, D), jnp.float32),
        compiler_params=_cparams(("arbitrary",)),
        name="combine",
    )(dest_flat, x1, gates, nw, rows)


def kernel(x, attn_norm_w, w_in, diff_lambda_q1, diff_lambda_k1, diff_lambda_q2, diff_lambda_k2,
           diff_subln_w, w_out, ffn_norm_w, router_w, router_b, w_gate_up, b_gate_up, w_down,
           b_down, final_norm_w):
    B, S, D = x.shape
    T = B * S
    f32 = jnp.float32
    n = jnp.arange(1, N_HEADS_TOTAL + 1, dtype=f32)
    slopes = jnp.exp2(-8.0 * n / N_HEADS_TOTAL)
    diff_slopes = slopes[0::2] * LOG2E
    dil_slopes = slopes[1::2] * LOG2E
    depth = attn_norm_w.shape[0]
    assert depth == 1, "the combine kernel fuses the final norm, so exactly one layer is supported"
    x2d = x.reshape(T, D)
    diff_w = N_DIFF_HEADS * HEAD_DIM
    colscale = jnp.concatenate([
        jnp.full((diff_w,), DIFF_QK_DIM ** -0.5 * LOG2E, f32), jnp.ones((2 * diff_w,), f32),
        jnp.full((N_DIL_HEADS * HEAD_DIM,), HEAD_DIM ** -0.5 * LOG2E, f32),
        jnp.ones((2 * N_DIL_HEADS * HEAD_DIM,), f32)]).reshape(1, IN_COLS)
    n_blocks = -(-(T * TOP_K + N_EXPERTS * (MOE_BLK - 1)) // MOE_BLK)
    R = n_blocks * MOE_BLK

    for l in range(depth):
        proj = _in_proj(x2d, attn_norm_w[l].reshape(1, D), w_in[l].astype(jnp.bfloat16), colscale)
        lam_init = 0.8 - 0.6 * math.exp(-0.3 * l)
        lam = (jnp.exp(jnp.sum(diff_lambda_q1[l] * diff_lambda_k1[l]).astype(f32))
               - jnp.exp(jnp.sum(diff_lambda_q2[l] * diff_lambda_k2[l]).astype(f32))
               + lam_init).reshape(1)
        subln2 = jnp.tile(diff_subln_w[l], 2).reshape(1, LANES)
        mix_d = _diff_attn(proj, diff_slopes, lam, subln2, B, S, 1.0 - lam_init)
        mix_l = _dil_attn(proj, dil_slopes, B, S)

        rw = jnp.zeros((D, LANES), f32).at[:, :N_EXPERTS].set(router_w[l])
        rw_hi = rw.astype(jnp.bfloat16)
        rw_lo = (rw - rw_hi.astype(f32)).astype(jnp.bfloat16)
        rb = jnp.zeros((1, LANES), f32).at[0, :N_EXPERTS].set(router_b[l])
        x1, h2, route, gates, cnt = _post_attn(
            x2d, mix_d, mix_l, w_out[l].astype(jnp.bfloat16), ffn_norm_w[l].reshape(1, D),
            jnp.concatenate([rw_hi, rw_lo], axis=1), rb)

        counts = cnt[0, :N_EXPERTS]
        pcounts = (counts + MOE_BLK - 1) // MOE_BLK * MOE_BLK
        pends = jnp.cumsum(pcounts)
        pstarts = pends - pcounts
        top_i = route[:, :TOP_K]
        rank = route[:, TOP_K:2 * TOP_K]
        onehot = top_i[..., None] == jnp.arange(N_EXPERTS, dtype=jnp.int32)
        dest = jnp.sum(jnp.where(onehot, pstarts.astype(jnp.int32), 0), axis=-1) + rank
        dest_flat = dest.reshape(T * TOP_K).astype(jnp.int32)
        used = (pends[-1] // MOE_BLK).astype(jnp.int32)
        blk_start = jnp.arange(n_blocks, dtype=jnp.int32) * MOE_BLK
        be = jnp.minimum(jnp.sum(pends[None, :] <= blk_start[:, None], axis=1), N_EXPERTS - 1)
        be_last = jnp.max(jnp.where(jnp.arange(n_blocks) < used, be, 0))
        block_e = jnp.where(jnp.arange(n_blocks) < used, be, be_last).astype(jnp.int32)

        xs = _dispatch(dest_flat, counts.astype(jnp.int32), pcounts.astype(jnp.int32),
                       pstarts.astype(jnp.int32), h2, R)
        rows = _experts(block_e, used.reshape(1), xs, w_gate_up[l], b_gate_up[l], w_down[l], b_down[l])
        x2d = _combine(dest_flat, x1, gates, final_norm_w.reshape(1, D), rows)
    return x2d.reshape(B, S, D)
```

```python
import functools
import math

import jax
import jax.numpy as jnp
from jax import lax
from jax.experimental import pallas as pl
from jax.experimental.pallas import tpu as pltpu

D_MODEL = 1024
HEAD_DIM = 64
N_DIFF_HEADS = 8
DIFF_QK_DIM = 32
N_DIL_HEADS = 8
N_HEADS_TOTAL = 16
DIL_PATTERNS = ((128, 1), (512, 4), (2048, 16))
DIL_W = 128
N_EXPERTS = 32
TOP_K = 4
D_FF = D_MODEL
SWIGLU_LIMIT = 7.0
SWIGLU_ALPHA = 1.702
NORM_EPS = 1e-5
IN_COLS = 3072
LANES = 128
SUBLANES = 8
LOG2E = math.log2(math.e)
MOE_BLK = 512
VMEM_LIMIT = 56 * 1024 * 1024

NEG_INF = float("-inf")


def _cparams(sem, vmem=VMEM_LIMIT):
    return pltpu.CompilerParams(dimension_semantics=sem, vmem_limit_bytes=vmem)


def _in_proj_kernel(x_ref, nw_ref, w_ref, cs_ref, o_ref, *, n_chunk):
    xf = x_ref[...]
    ms = jnp.mean(xf * xf, axis=-1, keepdims=True)
    h = (xf * lax.rsqrt(ms + NORM_EPS)) * nw_ref[...]
    hb = h.astype(jnp.bfloat16)
    n_cols = o_ref.shape[1]
    for c in range(n_cols // n_chunk):
        sl = slice(c * n_chunk, (c + 1) * n_chunk)
        r = jnp.dot(hb, w_ref[:, sl], preferred_element_type=jnp.float32)
        o_ref[:, sl] = (r * cs_ref[:, sl]).astype(o_ref.dtype)


def _in_proj(x2d, nw, w_bf16, colscale, tm=512):
    T, D = x2d.shape
    N = w_bf16.shape[1]
    return pl.pallas_call(
        functools.partial(_in_proj_kernel, n_chunk=512),
        grid=(T // tm,),
        in_specs=[
            pl.BlockSpec((tm, D), lambda i: (i, 0)),
            pl.BlockSpec((1, D), lambda i: (0, 0)),
            pl.BlockSpec((D, N), lambda i: (0, 0)),
            pl.BlockSpec((1, N), lambda i: (0, 0)),
        ],
        out_specs=pl.BlockSpec((tm, N), lambda i: (i, 0)),
        out_shape=jax.ShapeDtypeStruct((T, N), jnp.bfloat16),
        compiler_params=_cparams(("arbitrary",)),
        name="in_proj",
    )(x2d, nw, w_bf16, colscale)


def _diff_attn_kernel(slopes_ref, lam_ref, q_ref, k_ref, v_ref, w_ref, o_ref,
                      q4t_ref, pos_ref, vt_ref, mask_ref, s_ref, mt_ref, acc_ref,
                      *, tq, tk, post_scale):
    assert tq == tk
    p = pl.program_id(1)
    qi = pl.program_id(2)
    n_lanes = 4 * tq
    n_kv = v_ref.shape[0] // tk
    vt_rows = vt_ref.shape[2]

    lane = lax.broadcasted_iota(jnp.int32, (1, n_lanes), 1)
    slope_l = jnp.where(lane < 2 * tq, slopes_ref[2 * p], slopes_ref[2 * p + 1])

    @pl.when(qi == 0)
    def _():
        ones = jnp.ones((vt_rows - HEAD_DIM, tk), jnp.bfloat16)
        for jj in range(n_kv):
            vt = v_ref[jj * tk:(jj + 1) * tk, :].astype(jnp.float32).T.astype(jnp.bfloat16)
            for h in range(2):
                vt_ref[jj, h, 0:HEAD_DIM, :] = vt[h * HEAD_DIM:(h + 1) * HEAD_DIM, :]
                vt_ref[jj, h, HEAD_DIM:vt_rows, :] = ones
        hi = slope_l.astype(jnp.bfloat16).astype(jnp.float32)
        mid = (slope_l - hi).astype(jnp.bfloat16).astype(jnp.float32)
        lo = slope_l - hi - mid
        r16 = lax.broadcasted_iota(jnp.int32, (16, n_lanes), 0)
        rows = jnp.where(r16 == 0, hi, jnp.where(r16 == 1, mid, jnp.where(r16 == 2, lo, 0.0)))
        q4t_ref[LANES:LANES + 16, :] = rows.astype(jnp.bfloat16)
        q4t_ref[LANES + 16:, :] = jnp.zeros((LANES - 16, n_lanes), jnp.bfloat16)
        key_c = lax.broadcasted_iota(jnp.int32, (tk, LANES), 0)
        col_c = lax.broadcasted_iota(jnp.int32, (tk, LANES), 1)
        pos_ref[...] = jnp.where(col_c < 3, key_c, 0).astype(jnp.float32).astype(jnp.bfloat16)
        key_i = lax.broadcasted_iota(jnp.int32, (tk, n_lanes), 0)
        qry_i = lax.broadcasted_iota(jnp.int32, (tk, n_lanes), 1) & (tq - 1)
        mask_ref[...] = jnp.where(key_i <= qry_i, 0.0, NEG_INF)

    qt = q_ref[...].astype(jnp.float32).T
    sub = lax.broadcasted_iota(jnp.int32, (LANES, tq), 0) >> 5
    for c in range(4):
        q4t_ref[0:LANES, c * tq:(c + 1) * tq] = jnp.where(sub == c, qt, 0.0).astype(jnp.bfloat16)
    acc_ref[...] = jnp.zeros(acc_ref.shape, jnp.float32)

    def tile_off(n):
        return slope_l * jnp.asarray(n * tk).astype(jnp.float32)

    def scores(n, slot, mask):
        ka = jnp.concatenate([k_ref[n * tk:(n + 1) * tk, :], pos_ref[...]], axis=1)
        s = jnp.dot(ka, q4t_ref[...], preferred_element_type=jnp.float32)
        if mask == "diag":
            s = s + mask_ref[...]
        s_ref[slot] = s
        mt_ref[slot] = jnp.max(s, axis=0, keepdims=True) + tile_off(n)

    def weights(n, slot, m_prev):
        m_new = jnp.maximum(m_prev, mt_ref[slot])
        alpha = jnp.exp2(m_prev - m_new)
        pb = jnp.exp2(s_ref[slot] - (m_new - tile_off(n))).astype(jnp.bfloat16)
        for h in range(2):
            cols = slice(h * 2 * tq, (h + 1) * 2 * tq)
            pv = jnp.dot(vt_ref[n, h], pb[:, cols], preferred_element_type=jnp.float32)
            acc_ref[h] = alpha[:, cols] * acc_ref[h] + pv
        return m_new

    def run_tiles(last):
        scores(0, 0, "diag" if last == 0 else None)
        m = jnp.full((1, n_lanes), NEG_INF, jnp.float32)
        for t in range(last + 1):
            if t < last:
                scores(t + 1, (t + 1) & 1, "diag" if t + 1 == last else None)
            m = weights(t, t & 1, m)

    for c in range(n_kv):
        pl.when(qi == c)(functools.partial(run_tiles, c))

    lam = lam_ref[0]
    ys = []
    for h in range(2):
        acc = acc_ref[h]
        num = acc[0:HEAD_DIM]
        l = acc[HEAD_DIM:HEAD_DIM + 1]
        o = num[:, :tq] / l[:, :tq] - lam * (num[:, tq:] / l[:, tq:])
        ms = jnp.mean(o * o, axis=0, keepdims=True)
        ys.append(o * lax.rsqrt(ms + NORM_EPS))
    y = jnp.concatenate(ys, axis=0).T
    o_ref[...] = ((y * w_ref[...]) * post_scale).astype(o_ref.dtype)


def _diff_attn(proj, slopes, lam, subln_w2, B, S, post_scale, tq=256, tk=256):
    T = proj.shape[0]
    nq = S // tq
    n_pairs = N_DIFF_HEADS // 2
    kern = functools.partial(_diff_attn_kernel, tq=tq, tk=tk, post_scale=post_scale)
    return pl.pallas_call(
        kern,
        grid=(B, n_pairs, nq),
        in_specs=[
            pl.BlockSpec(memory_space=pltpu.SMEM),
            pl.BlockSpec(memory_space=pltpu.SMEM),
            pl.BlockSpec((tq, LANES), lambda b, p, i: (b * nq + i, p)),
            pl.BlockSpec((S, LANES), lambda b, p, i: (b, n_pairs + p)),
            pl.BlockSpec((S, LANES), lambda b, p, i: (b, 2 * n_pairs + p)),
            pl.BlockSpec((1, LANES), lambda b, p, i: (0, 0)),
        ],
        out_specs=pl.BlockSpec((tq, LANES), lambda b, p, i: (b * nq + i, p)),
        out_shape=jax.ShapeDtypeStruct((T, N_DIFF_HEADS * HEAD_DIM), jnp.bfloat16),
        scratch_shapes=[
            pltpu.VMEM((2 * LANES, 4 * tq), jnp.bfloat16),
            pltpu.VMEM((tk, LANES), jnp.bfloat16),
            pltpu.VMEM((S // tk, 2, HEAD_DIM + 16, tk), jnp.bfloat16),
            pltpu.VMEM((tk, 4 * tq), jnp.float32),
            pltpu.VMEM((2, tk, 4 * tq), jnp.float32),
            pltpu.VMEM((2, 1, 4 * tq), jnp.float32),
            pltpu.VMEM((2, HEAD_DIM + 16, 2 * tq), jnp.float32),
        ],
        compiler_params=_cparams(("arbitrary", "arbitrary", "arbitrary")),
        name="diff_attn",
    )(slopes, lam, proj, proj, proj, subln_w2)


def _dil_attn_kernel(slopes_ref, q_ref, k_ref, v_ref, o_ref,
                     qf_ref, kf_ref, vf_ref, qd_ref, kd_ref, vd_ref, a_ref,
                     po_ref, pl_ref, ro_ref, rl_ref, *, S):
    p = pl.program_id(1)
    W = DIL_W
    n_blocks = S // W
    qf_ref[...] = q_ref[...].astype(jnp.float32)
    kf_ref[...] = k_ref[...].astype(jnp.float32)
    vf_ref[...] = v_ref[...].astype(jnp.float32)
    @pl.when((pl.program_id(0) == 0) & (p == 0))
    def _():
        for i in range(len(DIL_PATTERNS)):
            kd_ref[i, 0:W, :] = jnp.zeros((W, LANES), jnp.bfloat16)
            vd_ref[i, 0:W, 0:LANES] = jnp.zeros((W, LANES), jnp.bfloat16)
            vd_ref[i, :, LANES:] = jnp.ones((S + W, LANES), jnp.bfloat16)

    lane = lax.broadcasted_iota(jnp.int32, (W, LANES), 1)
    first = lane < HEAD_DIM
    r_i = lax.broadcasted_iota(jnp.int32, (W, 2 * W), 0)
    c_i = lax.broadcasted_iota(jnp.int32, (W, 2 * W), 1)
    dist = W + r_i - c_i
    valid = (dist >= 0) & (dist <= W)
    dist_f = dist.astype(jnp.float32)

    for pi, (_, d) in enumerate(DIL_PATTERNS):
        L = S // d
        nbk = L // W
        qd, kd, vd, bias = qd_ref.at[pi], kd_ref.at[pi], vd_ref.at[pi], a_ref.at[pi]
        po, plse = po_ref.at[pi], pl_ref.at[pi]
        if d == 1:
            qd[...] = q_ref[...]
            kd[W:W + S, :] = k_ref[...]
            vd[W:W + S, 0:LANES] = v_ref[...]
        else:
            for r in range(d):
                qd[r * L:(r + 1) * L, :] = qf_ref[pl.ds(r, L, stride=d), :].astype(jnp.bfloat16)
                kd[W + r * L:W + (r + 1) * L, :] = kf_ref[pl.ds(r, L, stride=d), :].astype(jnp.bfloat16)
                vd[W + r * L:W + (r + 1) * L, 0:LANES] = vf_ref[pl.ds(r, L, stride=d), :].astype(jnp.bfloat16)
        for h in range(2):
            slope = slopes_ref[2 * p + h] * float(d)
            full = jnp.where(valid, slope * dist_f, jnp.inf)
            bias[0, h * W:(h + 1) * W, :] = full
            bias[1, h * W:(h + 1) * W, :] = jnp.where(c_i < W, jnp.inf, full)

        dst_o = ro_ref.at[pi] if d == 1 else po
        dst_l = rl_ref.at[pi] if d == 1 else plse

        def one_block(u, no_prev, dst_o=dst_o, dst_l=dst_l, qd=qd, kd=kd, vd=vd, bias=bias):
            row0 = u * W
            qb = qd[pl.ds(row0, W), :].astype(jnp.float32)
            kk = kd[pl.ds(row0, 2 * W), :]
            vv = vd[pl.ds(row0, 2 * W), :]
            q2 = jnp.concatenate([jnp.where(first, qb, 0.0), jnp.where(first, 0.0, qb)],
                                 axis=0).astype(jnp.bfloat16)
            s = lax.dot_general(q2, kk, (((1,), (1,)), ((), ())),
                                preferred_element_type=jnp.float32)
            s = s - bias[no_prev]
            m = jnp.max(s, axis=1, keepdims=True)
            pr = jnp.exp2(s - m)
            pv = jnp.dot(pr.astype(jnp.bfloat16), vv, preferred_element_type=jnp.float32)
            l = pv[:, LANES:]
            o = pv[:, :LANES] / l
            lse = m + jnp.log2(l)
            dst_o[pl.ds(row0, W), :] = jnp.where(first, o[:W], o[W:])
            dst_l[pl.ds(row0, W), :] = jnp.where(first, lse[:W], lse[W:])

        for u in range(n_blocks):
            one_block(u, int(u % nbk == 0))

        if d > 1:
            for r in range(d):
                ro_ref[pi, pl.ds(r, L, stride=d), :] = po[r * L:(r + 1) * L, :]
                rl_ref[pi, pl.ds(r, L, stride=d), :] = plse[r * L:(r + 1) * L, :]

    def mix(c, carry):
        row0 = pl.multiple_of(c * W, W)
        ls = [rl_ref[i, pl.ds(row0, W), :] for i in range(3)]
        mx = jnp.maximum(jnp.maximum(ls[0], ls[1]), ls[2])
        es = [jnp.exp2(x - mx) for x in ls]
        den = es[0] + es[1] + es[2]
        num = (es[0] * ro_ref[0, pl.ds(row0, W), :] + es[1] * ro_ref[1, pl.ds(row0, W), :]
               + es[2] * ro_ref[2, pl.ds(row0, W), :])
        o_ref[pl.ds(row0, W), :] = (num / den).astype(o_ref.dtype)
        return carry

    lax.fori_loop(0, n_blocks, mix, 0)


def _dil_attn(proj, slopes, B, S):
    T = proj.shape[0]
    n_pairs = N_DIL_HEADS // 2
    base = 3 * N_DIFF_HEADS * HEAD_DIM // LANES
    W = DIL_W
    return pl.pallas_call(
        functools.partial(_dil_attn_kernel, S=S),
        grid=(B, n_pairs),
        in_specs=[
            pl.BlockSpec(memory_space=pltpu.SMEM),
            pl.BlockSpec((S, LANES), lambda b, p: (b, base + p)),
            pl.BlockSpec((S, LANES), lambda b, p: (b, base + n_pairs + p)),
            pl.BlockSpec((S, LANES), lambda b, p: (b, base + 2 * n_pairs + p)),
        ],
        out_specs=pl.BlockSpec((S, LANES), lambda b, p: (b, p)),
        out_shape=jax.ShapeDtypeStruct((T, N_DIL_HEADS * HEAD_DIM), jnp.bfloat16),
        scratch_shapes=[
            pltpu.VMEM((S, LANES), jnp.float32),
            pltpu.VMEM((S, LANES), jnp.float32),
            pltpu.VMEM((S, LANES), jnp.float32),
            pltpu.VMEM((3, S, LANES), jnp.bfloat16),
            pltpu.VMEM((3, S + W, LANES), jnp.bfloat16),
            pltpu.VMEM((3, S + W, 2 * LANES), jnp.bfloat16),
            pltpu.VMEM((3, 2, 2 * W, 2 * W), jnp.float32),
            pltpu.VMEM((3, S, LANES), jnp.float32),
            pltpu.VMEM((3, S, LANES), jnp.float32),
            pltpu.VMEM((3, S, LANES), jnp.float32),
            pltpu.VMEM((3, S, LANES), jnp.float32),
        ],
        compiler_params=_cparams(("arbitrary", "arbitrary")),
        name="dil_attn",
    )(slopes, proj, proj, proj)


def _post_attn_kernel(x_ref, md_ref, ml_ref, wo_ref, nw_ref, rw_ref, rb_ref,
                      x1_ref, h2_ref, route_ref, gate_ref, cnt_ref, run_ref, tri_ref, *, tm):
    i = pl.program_id(0)

    @pl.when(i == 0)
    def _():
        run_ref[...] = jnp.zeros(run_ref.shape, jnp.float32)
        r_i = lax.broadcasted_iota(jnp.int32, (tm, tm), 0)
        c_i = lax.broadcasted_iota(jnp.int32, (tm, tm), 1)
        tri_ref[...] = jnp.where(c_i < r_i, 1.0, 0.0).astype(jnp.bfloat16)

    half = md_ref.shape[1]
    y = jnp.dot(md_ref[...], wo_ref[0:half, :], preferred_element_type=jnp.float32)
    y = y + jnp.dot(ml_ref[...], wo_ref[half:, :], preferred_element_type=jnp.float32)
    x1 = x_ref[...] + y
    x1_ref[...] = x1
    ms = jnp.mean(x1 * x1, axis=-1, keepdims=True)
    h2 = (x1 * lax.rsqrt(ms + NORM_EPS)) * nw_ref[...]
    for c in range(D_MODEL // LANES):
        h2_ref[pl.ds(c, tm, stride=SUBLANES), :] = h2[:, c * LANES:(c + 1) * LANES]
    hh = h2.astype(jnp.bfloat16)
    hl = (h2 - hh.astype(jnp.float32)).astype(jnp.bfloat16)
    lh = jnp.dot(hh, rw_ref[...], preferred_element_type=jnp.float32)
    ll = jnp.dot(hl, rw_ref[...], preferred_element_type=jnp.float32)
    logits = (lh[:, :LANES] + lh[:, LANES:]) + (ll[:, :LANES] + ll[:, LANES:]) + rb_ref[...]
    lane = lax.broadcasted_iota(jnp.int32, (tm, LANES), 1)
    work = jnp.where(lane < N_EXPERTS, logits, NEG_INF)
    vals, idxs, sels = [], [], []
    for _ in range(TOP_K):
        mk = jnp.max(work, axis=1, keepdims=True)
        ik = jnp.min(jnp.where(work == mk, lane, LANES), axis=1, keepdims=True)
        sel = lane == ik
        work = jnp.where(sel, NEG_INF, work)
        vals.append(mk)
        idxs.append(ik)
        sels.append(sel)
    es = [jnp.exp(v - vals[0]) for v in vals]
    den = es[0] + es[1] + es[2] + es[3]
    member = jnp.zeros((tm, LANES), jnp.float32)
    for sel in sels:
        member = jnp.where(sel, 1.0, member)
    before = jnp.dot(tri_ref[...], member.astype(jnp.bfloat16), preferred_element_type=jnp.float32)
    before = before + run_ref[...]
    route = jnp.zeros((tm, LANES), jnp.int32)
    gates = jnp.zeros((tm, LANES), jnp.float32)
    for k in range(TOP_K):
        rank = jnp.sum(jnp.where(sels[k], before, 0.0), axis=1, keepdims=True)
        route = jnp.where(lane == k, idxs[k], route)
        route = jnp.where(lane == TOP_K + k, rank.astype(jnp.int32), route)
        gates = jnp.where(lane == k, es[k] / den, gates)
    route_ref[...] = route
    gate_ref[...] = gates
    run_new = run_ref[...] + jnp.sum(member, axis=0, keepdims=True)
    run_ref[...] = run_new
    cnt_ref[...] = jnp.broadcast_to(run_new, cnt_ref.shape).astype(jnp.int32)


def _post_attn(x2d, mix_d, mix_l, wo_bf16, nw, rw_cat, rb, tm=512):
    T, D = x2d.shape
    half = mix_d.shape[1]
    return pl.pallas_call(
        functools.partial(_post_attn_kernel, tm=tm),
        grid=(T // tm,),
        in_specs=[
            pl.BlockSpec((tm, D), lambda i: (i, 0)),
            pl.BlockSpec((tm, half), lambda i: (i, 0)),
            pl.BlockSpec((tm, half), lambda i: (i, 0)),
            pl.BlockSpec((D, D), lambda i: (0, 0)),
            pl.BlockSpec((1, D), lambda i: (0, 0)),
            pl.BlockSpec((D, 2 * LANES), lambda i: (0, 0)),
            pl.BlockSpec((1, LANES), lambda i: (0, 0)),
        ],
        out_specs=[
            pl.BlockSpec((tm, D), lambda i: (i, 0)),
            pl.BlockSpec((tm * SUBLANES, LANES), lambda i: (i, 0)),
            pl.BlockSpec((tm, LANES), lambda i: (i, 0)),
            pl.BlockSpec((tm, LANES), lambda i: (i, 0)),
            pl.BlockSpec((8, LANES), lambda i: (0, 0)),
        ],
        out_shape=[
            jax.ShapeDtypeStruct((T, D), jnp.float32),
            jax.ShapeDtypeStruct((T * SUBLANES, LANES), jnp.float32),
            jax.ShapeDtypeStruct((T, LANES), jnp.int32),
            jax.ShapeDtypeStruct((T, LANES), jnp.float32),
            jax.ShapeDtypeStruct((8, LANES), jnp.int32),
        ],
        scratch_shapes=[pltpu.VMEM((1, LANES), jnp.float32), pltpu.VMEM((tm, tm), jnp.bfloat16)],
        compiler_params=_cparams(("arbitrary",)),
        name="post_attn",
    )(x2d, mix_d, mix_l, wo_bf16, nw, rw_cat, rb)


def _token_rows(ref, t, n_tokens=1):
    start = pl.multiple_of(t * SUBLANES, SUBLANES)
    return ref.at[pl.ds(start, n_tokens * SUBLANES), :]


def _dispatch_kernel(dest_ref, cnt_ref, pcnt_ref, pst_ref, h_ref, xs_ref, z_ref, sem, zsem, *, td):
    i = pl.program_id(0)
    n = pl.num_programs(0)

    def issue(t, carry):
        for k in range(TOP_K):
            pltpu.make_async_copy(_token_rows(h_ref, t),
                                  _token_rows(xs_ref, dest_ref[(i * td + t) * TOP_K + k]),
                                  sem).start(priority=k % 2)
        return carry

    lax.fori_loop(0, td, issue, 0)

    for k in range(TOP_K):
        pltpu.make_async_copy(h_ref, _token_rows(xs_ref, 0, td), sem).wait()

    @pl.when(i == n - 1)
    def _():
        z_ref[...] = jnp.zeros(z_ref.shape, z_ref.dtype)

        def pad_copy(row):
            return pltpu.make_async_copy(_token_rows(z_ref, 0), _token_rows(xs_ref, row), zsem)

        def per_expert(e, carry):
            base = pst_ref[e]

            def start(r, c):
                pad_copy(base + r).start()
                return c

            def wait(r, c):
                pad_copy(base + r).wait()
                return c

            lax.fori_loop(cnt_ref[e], pcnt_ref[e], start, 0)
            lax.fori_loop(cnt_ref[e], pcnt_ref[e], wait, 0)
            return carry

        lax.fori_loop(0, N_EXPERTS, per_expert, 0)

        def tail_copy(c):
            return pltpu.make_async_copy(z_ref, _token_rows(xs_ref, c * td, td), zsem)

        end = pst_ref[N_EXPERTS - 1] + pcnt_ref[N_EXPERTS - 1]
        first_chunk = end // td
        n_chunks = xs_ref.shape[0] // (td * SUBLANES)

        def tail_start(c, carry):
            tail_copy(c).start()
            return carry

        def tail_wait(c, carry):
            tail_copy(c).wait()
            return carry

        lax.fori_loop(first_chunk, n_chunks, tail_start, 0)
        lax.fori_loop(first_chunk, n_chunks, tail_wait, 0)


def _dispatch(dest_flat, counts, pcounts, pstarts, h2t, R, td=256):
    T = h2t.shape[0] // SUBLANES
    return pl.pallas_call(
        functools.partial(_dispatch_kernel, td=td),
        grid_spec=pltpu.PrefetchScalarGridSpec(
            num_scalar_prefetch=4,
            grid=(T // td,),
            in_specs=[pl.BlockSpec((td * SUBLANES, LANES), lambda i, *_: (i, 0))],
            out_specs=pl.BlockSpec(memory_space=pl.ANY),
            scratch_shapes=[
                pltpu.VMEM((td * SUBLANES, LANES), h2t.dtype),
                pltpu.SemaphoreType.DMA,
                pltpu.SemaphoreType.DMA,
            ],
        ),
        out_shape=jax.ShapeDtypeStruct((R * SUBLANES, LANES), h2t.dtype),
        compiler_params=_cparams(("arbitrary",)),
        name="dispatch",
    )(dest_flat, counts, pcounts, pstarts, h2t)


def _experts_kernel(be_ref, used_ref, xs_ref, wgu_ref, bgu_ref, wd_ref, bd_ref, o_ref,
                    wgu_bf, wd_bf, *, n_chunk):
    i = pl.program_id(0)
    prev = be_ref[jnp.maximum(i - 1, 0)]
    new_expert = jnp.logical_or(i == 0, be_ref[i] != prev)

    @pl.when(new_expert)
    def _():
        wgu_bf[...] = wgu_ref[0].astype(jnp.bfloat16)
        wd_bf[...] = wd_ref[0].astype(jnp.bfloat16)

    blk = xs_ref.shape[0] // SUBLANES
    n_lane_chunks = D_MODEL // LANES

    @pl.when(i < used_ref[0])
    def _():
        xb = jnp.concatenate(
            [xs_ref[pl.ds(c, blk, stride=SUBLANES), :].astype(jnp.bfloat16)
             for c in range(n_lane_chunks)], axis=1)
        acc = jnp.zeros((blk, D_MODEL), jnp.float32)
        for c in range(D_FF // n_chunk):
            gs = slice(c * n_chunk, (c + 1) * n_chunk)
            us = slice(D_FF + c * n_chunk, D_FF + (c + 1) * n_chunk)
            g = jnp.dot(xb, wgu_bf[:, gs], preferred_element_type=jnp.float32) + bgu_ref[0, :, gs]
            u = jnp.dot(xb, wgu_bf[:, us], preferred_element_type=jnp.float32) + bgu_ref[0, :, us]
            g = jnp.minimum(g, SWIGLU_LIMIT)
            u = jnp.clip(u, -SWIGLU_LIMIT, SWIGLU_LIMIT)
            act = (u + 1.0) * (g * jax.nn.sigmoid(SWIGLU_ALPHA * g))
            acc = acc + jnp.dot(act.astype(jnp.bfloat16), wd_bf[gs, :],
                                preferred_element_type=jnp.float32)
        res = acc + bd_ref[0]
        for c in range(n_lane_chunks):
            o_ref[pl.ds(c, blk, stride=SUBLANES), :] = res[:, c * LANES:(c + 1) * LANES]

    @pl.when(i >= used_ref[0])
    def _():
        o_ref[...] = jnp.zeros(o_ref.shape, o_ref.dtype)


def _experts(block_e, used, xs, w_gate_up, b_gate_up, w_down, b_down, blk=MOE_BLK):
    R = xs.shape[0] // SUBLANES
    D = D_MODEL
    n_blocks = R // blk
    F2 = w_gate_up.shape[2]

    def row_map(i, be, used):
        return (i, 0)

    return pl.pallas_call(
        functools.partial(_experts_kernel, n_chunk=512),
        grid_spec=pltpu.PrefetchScalarGridSpec(
            num_scalar_prefetch=2,
            grid=(n_blocks,),
            in_specs=[
                pl.BlockSpec((blk * SUBLANES, LANES), row_map),
                pl.BlockSpec((1, D, F2), lambda i, be, used: (be[i], 0, 0)),
                pl.BlockSpec((1, 1, F2), lambda i, be, used: (be[i], 0, 0)),
                pl.BlockSpec((1, D_FF, D), lambda i, be, used: (be[i], 0, 0)),
                pl.BlockSpec((1, 1, D), lambda i, be, used: (be[i], 0, 0)),
            ],
            out_specs=pl.BlockSpec((blk * SUBLANES, LANES), row_map),
            scratch_shapes=[
                pltpu.VMEM((D, F2), jnp.bfloat16),
                pltpu.VMEM((D_FF, D), jnp.bfloat16),
            ],
        ),
        out_shape=jax.ShapeDtypeStruct((R * SUBLANES, LANES), jnp.float32),
        compiler_params=_cparams(("arbitrary",)),
        name="experts",
    )(block_e, used, xs, w_gate_up, b_gate_up.reshape(N_EXPERTS, 1, F2),
      w_down, b_down.reshape(N_EXPERTS, 1, D))


def _combine_kernel(dest_ref, x1_ref, g_ref, nw_ref, rows_ref, o_ref, buf_ref, sems, *, tc):
    i = pl.program_id(0)
    n = pl.num_programs(0)
    slot = i & 1

    def issue_tile(tile, s):
        def issue(t, carry):
            for k in range(TOP_K):
                d = dest_ref[(tile * tc + t) * TOP_K + k]
                pltpu.make_async_copy(_token_rows(rows_ref, d), _token_rows(buf_ref.at[s, k], t),
                                      sems.at[s]).start(priority=k % 2)
            return carry

        lax.fori_loop(0, tc, issue, 0)

    @pl.when(i == 0)
    def _():
        issue_tile(0, 0)

    for s in range(2):
        @pl.when((i + 1 < n) & (slot == 1 - s))
        def _(s=s):
            issue_tile(i + 1, s)

    for k in range(TOP_K):
        pltpu.make_async_copy(_token_rows(rows_ref, 0, tc), buf_ref.at[slot, k],
                              sems.at[slot]).wait()

    g = g_ref[...]
    x1 = x1_ref[...]
    chunks = []
    for c in range(D_MODEL // LANES):
        yc = x1[:, c * LANES:(c + 1) * LANES]
        for k in range(TOP_K):
            yc = yc + g[:, k:k + 1] * buf_ref[slot, k, pl.ds(c, tc, stride=SUBLANES), :]
        chunks.append(yc)
    y = jnp.concatenate(chunks, axis=1)
    ms = jnp.mean(y * y, axis=-1, keepdims=True)
    o_ref[...] = (y * lax.rsqrt(ms + NORM_EPS)) * nw_ref[...]


def _combine(dest_flat, x1, gates, nw, rows, tc=256):
    T, D = x1.shape
    return pl.pallas_call(
        functools.partial(_combine_kernel, tc=tc),
        grid_spec=pltpu.PrefetchScalarGridSpec(
            num_scalar_prefetch=1,
            grid=(T // tc,),
            in_specs=[
                pl.BlockSpec((tc, D), lambda i, *_: (i, 0)),
                pl.BlockSpec((tc, LANES), lambda i, *_: (i, 0)),
                pl.BlockSpec((1, D), lambda i, *_: (0, 0)),
                pl.BlockSpec(memory_space=pl.ANY),
            ],
            out_specs=pl.BlockSpec((tc, D), lambda i, *_: (i, 0)),
            scratch_shapes=[
                pltpu.VMEM((2, TOP_K, tc * SUBLANES, LANES), jnp.float32),
                pltpu.SemaphoreType.DMA((2,)),
            ],
        ),
        out_shape=jax.ShapeDtypeStruct((T, D), jnp.float32),
        compiler_params=_cparams(("arbitrary",)),
        name="combine",
    )(dest_flat, x1, gates, nw, rows)


def kernel(x, attn_norm_w, w_in, diff_lambda_q1, diff_lambda_k1, diff_lambda_q2, diff_lambda_k2,
           diff_subln_w, w_out, ffn_norm_w, router_w, router_b, w_gate_up, b_gate_up, w_down,
           b_down, final_norm_w):
    B, S, D = x.shape
    T = B * S
    f32 = jnp.float32
    n = jnp.arange(1, N_HEADS_TOTAL + 1, dtype=f32)
    slopes = jnp.exp2(-8.0 * n / N_HEADS_TOTAL)
    diff_slopes = slopes[0::2] * LOG2E
    dil_slopes = slopes[1::2] * LOG2E
    depth = attn_norm_w.shape[0]
    assert depth == 1, "the combine kernel fuses the final norm, so exactly one layer is supported"
    x2d = x.reshape(T, D)
    diff_w = N_DIFF_HEADS * HEAD_DIM
    colscale = jnp.concatenate([
        jnp.full((diff_w,), DIFF_QK_DIM ** -0.5 * LOG2E, f32), jnp.ones((2 * diff_w,), f32),
        jnp.full((N_DIL_HEADS * HEAD_DIM,), HEAD_DIM ** -0.5 * LOG2E, f32),
        jnp.ones((2 * N_DIL_HEADS * HEAD_DIM,), f32)]).reshape(1, IN_COLS)
    n_blocks = -(-(T * TOP_K + N_EXPERTS * (MOE_BLK - 1)) // MOE_BLK)
    R = n_blocks * MOE_BLK

    for l in range(depth):
        proj = _in_proj(x2d, attn_norm_w[l].reshape(1, D), w_in[l].astype(jnp.bfloat16), colscale)
        lam_init = 0.8 - 0.6 * math.exp(-0.3 * l)
        lam = (jnp.exp(jnp.sum(diff_lambda_q1[l] * diff_lambda_k1[l]).astype(f32))
               - jnp.exp(jnp.sum(diff_lambda_q2[l] * diff_lambda_k2[l]).astype(f32))
               + lam_init).reshape(1)
        subln2 = jnp.tile(diff_subln_w[l], 2).reshape(1, LANES)
        mix_d = _diff_attn(proj, diff_slopes, lam, subln2, B, S, 1.0 - lam_init)
        mix_l = _dil_attn(proj, dil_slopes, B, S)

        rw = jnp.zeros((D, LANES), f32).at[:, :N_EXPERTS].set(router_w[l])
        rw_hi = rw.astype(jnp.bfloat16)
        rw_lo = (rw - rw_hi.astype(f32)).astype(jnp.bfloat16)
        rb = jnp.zeros((1, LANES), f32).at[0, :N_EXPERTS].set(router_b[l])
        x1, h2, route, gates, cnt = _post_attn(
            x2d, mix_d, mix_l, w_out[l].astype(jnp.bfloat16), ffn_norm_w[l].reshape(1, D),
            jnp.concatenate([rw_hi, rw_lo], axis=1), rb)

        counts = cnt[0, :N_EXPERTS]
        pcounts = (counts + MOE_BLK - 1) // MOE_BLK * MOE_BLK
        pends = jnp.cumsum(pcounts)
        pstarts = pends - pcounts
        top_i = route[:, :TOP_K]
        rank = route[:, TOP_K:2 * TOP_K]
        onehot = top_i[..., None] == jnp.arange(N_EXPERTS, dtype=jnp.int32)
        dest = jnp.sum(jnp.where(onehot, pstarts.astype(jnp.int32), 0), axis=-1) + rank
        dest_flat = dest.reshape(T * TOP_K).astype(jnp.int32)
        used = (pends[-1] // MOE_BLK).astype(jnp.int32)
        blk_start = jnp.arange(n_blocks, dtype=jnp.int32) * MOE_BLK
        be = jnp.minimum(jnp.sum(pends[None, :] <= blk_start[:, None], axis=1), N_EXPERTS - 1)
        be_last = jnp.max(jnp.where(jnp.arange(n_blocks) < used, be, 0))
        block_e = jnp.where(jnp.arange(n_blocks) < used, be, be_last).astype(jnp.int32)

        xs = _dispatch(dest_flat, counts.astype(jnp.int32), pcounts.astype(jnp.int32),
                       pstarts.astype(jnp.int32), h2, R)
        rows = _experts(block_e, used.reshape(1), xs, w_gate_up[l], b_gate_up[l], w_down[l], b_down[l])
        x2d = _combine(dest_flat, x1, gates, final_norm_w.reshape(1, D), rows)
    return x2d.reshape(B, S, D)
```

```python
import functools
import math

import jax
import jax.numpy as jnp
from jax import lax
from jax.experimental import pallas as pl
from jax.experimental.pallas import tpu as pltpu

D_MODEL = 1024
HEAD_DIM = 64
N_DIFF_HEADS = 8
DIFF_QK_DIM = 32
N_DIL_HEADS = 8
N_HEADS_TOTAL = 16
DIL_PATTERNS = ((128, 1), (512, 4), (2048, 16))
DIL_W = 128
N_EXPERTS = 32
TOP_K = 4
D_FF = D_MODEL
SWIGLU_LIMIT = 7.0
SWIGLU_ALPHA = 1.702
NORM_EPS = 1e-5
IN_COLS = 3072
LANES = 128
SUBLANES = 8
LOG2E = math.log2(math.e)
MOE_BLK = 512
VMEM_LIMIT = 56 * 1024 * 1024

NEG_INF = float("-inf")


def _cparams(sem, vmem=VMEM_LIMIT):
    return pltpu.CompilerParams(dimension_semantics=sem, vmem_limit_bytes=vmem)


def _in_proj_kernel(x_ref, nw_ref, w_ref, cs_ref, o_ref, *, n_chunk):
    xf = x_ref[...]
    ms = jnp.mean(xf * xf, axis=-1, keepdims=True)
    h = (xf * lax.rsqrt(ms + NORM_EPS)) * nw_ref[...]
    hb = h.astype(jnp.bfloat16)
    n_cols = o_ref.shape[1]
    for c in range(n_cols // n_chunk):
        sl = slice(c * n_chunk, (c + 1) * n_chunk)
        r = jnp.dot(hb, w_ref[:, sl], preferred_element_type=jnp.float32)
        o_ref[:, sl] = (r * cs_ref[:, sl]).astype(o_ref.dtype)


def _in_proj(x2d, nw, w_bf16, colscale, tm=512):
    T, D = x2d.shape
    N = w_bf16.shape[1]
    return pl.pallas_call(
        functools.partial(_in_proj_kernel, n_chunk=512),
        grid=(T // tm,),
        in_specs=[
            pl.BlockSpec((tm, D), lambda i: (i, 0)),
            pl.BlockSpec((1, D), lambda i: (0, 0)),
            pl.BlockSpec((D, N), lambda i: (0, 0)),
            pl.BlockSpec((1, N), lambda i: (0, 0)),
        ],
        out_specs=pl.BlockSpec((tm, N), lambda i: (i, 0)),
        out_shape=jax.ShapeDtypeStruct((T, N), jnp.bfloat16),
        compiler_params=_cparams(("arbitrary",)),
        name="in_proj",
    )(x2d, nw, w_bf16, colscale)


def _diff_attn_kernel(slopes_ref, lam_ref, q_ref, k_ref, v_ref, w_ref, o_ref,
                      q4t_ref, pos_ref, vt_ref, mask_ref, s_ref, mt_ref, acc_ref,
                      *, tq, tk, post_scale):
    assert tq == tk
    p = pl.program_id(1)
    qi = pl.program_id(2)
    n_lanes = 4 * tq
    n_kv = v_ref.shape[0] // tk
    vt_rows = vt_ref.shape[2]

    lane = lax.broadcasted_iota(jnp.int32, (1, n_lanes), 1)
    slope_l = jnp.where(lane < 2 * tq, slopes_ref[2 * p], slopes_ref[2 * p + 1])

    @pl.when(qi == 0)
    def _():
        ones = jnp.ones((vt_rows - HEAD_DIM, tk), jnp.bfloat16)
        for jj in range(n_kv):
            vt = v_ref[jj * tk:(jj + 1) * tk, :].astype(jnp.float32).T.astype(jnp.bfloat16)
            for h in range(2):
                vt_ref[jj, h, 0:HEAD_DIM, :] = vt[h * HEAD_DIM:(h + 1) * HEAD_DIM, :]
                vt_ref[jj, h, HEAD_DIM:vt_rows, :] = ones
        hi = slope_l.astype(jnp.bfloat16).astype(jnp.float32)
        mid = (slope_l - hi).astype(jnp.bfloat16).astype(jnp.float32)
        lo = slope_l - hi - mid
        r16 = lax.broadcasted_iota(jnp.int32, (16, n_lanes), 0)
        rows = jnp.where(r16 == 0, hi, jnp.where(r16 == 1, mid, jnp.where(r16 == 2, lo, 0.0)))
        q4t_ref[LANES:LANES + 16, :] = rows.astype(jnp.bfloat16)
        q4t_ref[LANES + 16:, :] = jnp.zeros((LANES - 16, n_lanes), jnp.bfloat16)
        key_c = lax.broadcasted_iota(jnp.int32, (tk, LANES), 0)
        col_c = lax.broadcasted_iota(jnp.int32, (tk, LANES), 1)
        pos_ref[...] = jnp.where(col_c < 3, key_c, 0).astype(jnp.float32).astype(jnp.bfloat16)
        key_i = lax.broadcasted_iota(jnp.int32, (tk, n_lanes), 0)
        qry_i = lax.broadcasted_iota(jnp.int32, (tk, n_lanes), 1) & (tq - 1)
        mask_ref[...] = jnp.where(key_i <= qry_i, 0.0, NEG_INF)

    qt = q_ref[...].astype(jnp.float32).T
    sub = lax.broadcasted_iota(jnp.int32, (LANES, tq), 0) >> 5
    for c in range(4):
        q4t_ref[0:LANES, c * tq:(c + 1) * tq] = jnp.where(sub == c, qt, 0.0).astype(jnp.bfloat16)
    acc_ref[...] = jnp.zeros(acc_ref.shape, jnp.float32)

    def tile_off(n):
        return slope_l * jnp.asarray(n * tk).astype(jnp.float32)

    def scores(n, slot, mask):
        ka = jnp.concatenate([k_ref[n * tk:(n + 1) * tk, :], pos_ref[...]], axis=1)
        s = jnp.dot(ka, q4t_ref[...], preferred_element_type=jnp.float32)
        if mask == "diag":
            s = s + mask_ref[...]
        s_ref[slot] = s
        mt_ref[slot] = jnp.max(s, axis=0, keepdims=True) + tile_off(n)

    def weights(n, slot, m_prev):
        m_new = jnp.maximum(m_prev, mt_ref[slot])
        alpha = jnp.exp2(m_prev - m_new)
        pb = jnp.exp2(s_ref[slot] - (m_new - tile_off(n))).astype(jnp.bfloat16)
        for h in range(2):
            cols = slice(h * 2 * tq, (h + 1) * 2 * tq)
            pv = jnp.dot(vt_ref[n, h], pb[:, cols], preferred_element_type=jnp.float32)
            acc_ref[h] = alpha[:, cols] * acc_ref[h] + pv
        return m_new

    def run_tiles(last):
        scores(0, 0, "diag" if last == 0 else None)
        m = jnp.full((1, n_lanes), NEG_INF, jnp.float32)
        for t in range(last + 1):
            if t < last:
                scores(t + 1, (t + 1) & 1, "diag" if t + 1 == last else None)
            m = weights(t, t & 1, m)

    for c in range(n_kv):
        pl.when(qi == c)(functools.partial(run_tiles, c))

    lam = lam_ref[0]
    ys = []
    for h in range(2):
        acc = acc_ref[h]
        num = acc[0:HEAD_DIM]
        l = acc[HEAD_DIM:HEAD_DIM + 1]
        o = num[:, :tq] / l[:, :tq] - lam * (num[:, tq:] / l[:, tq:])
        ms = jnp.mean(o * o, axis=0, keepdims=True)
        ys.append(o * lax.rsqrt(ms + NORM_EPS))
    y = jnp.concatenate(ys, axis=0).T
    o_ref[...] = ((y * w_ref[...]) * post_scale).astype(o_ref.dtype)


def _diff_attn(proj, slopes, lam, subln_w2, B, S, post_scale, tq=256, tk=256):
    T = proj.shape[0]
    nq = S // tq
    n_pairs = N_DIFF_HEADS // 2
    kern = functools.partial(_diff_attn_kernel, tq=tq, tk=tk, post_scale=post_scale)
    return pl.pallas_call(
        kern,
        grid=(B, n_pairs, nq),
        in_specs=[
            pl.BlockSpec(memory_space=pltpu.SMEM),
            pl.BlockSpec(memory_space=pltpu.SMEM),
            pl.BlockSpec((tq, LANES), lambda b, p, i: (b * nq + i, p)),
            pl.BlockSpec((S, LANES), lambda b, p, i: (b, n_pairs + p)),
            pl.BlockSpec((S, LANES), lambda b, p, i: (b, 2 * n_pairs + p)),
            pl.BlockSpec((1, LANES), lambda b, p, i: (0, 0)),
        ],
        out_specs=pl.BlockSpec((tq, LANES), lambda b, p, i: (b * nq + i, p)),
        out_shape=jax.ShapeDtypeStruct((T, N_DIFF_HEADS * HEAD_DIM), jnp.bfloat16),
        scratch_shapes=[
            pltpu.VMEM((2 * LANES, 4 * tq), jnp.bfloat16),
            pltpu.VMEM((tk, LANES), jnp.bfloat16),
            pltpu.VMEM((S // tk, 2, HEAD_DIM + 16, tk), jnp.bfloat16),
            pltpu.VMEM((tk, 4 * tq), jnp.float32),
            pltpu.VMEM((2, tk, 4 * tq), jnp.float32),
            pltpu.VMEM((2, 1, 4 * tq), jnp.float32),
            pltpu.VMEM((2, HEAD_DIM + 16, 2 * tq), jnp.float32),
        ],
        compiler_params=_cparams(("arbitrary", "arbitrary", "arbitrary")),
        name="diff_attn",
    )(slopes, lam, proj, proj, proj, subln_w2)


def _dil_attn_kernel(slopes_ref, q_ref, k_ref, v_ref, o_ref,
                     qf_ref, kf_ref, vf_ref, qd_ref, kd_ref, vd_ref, a_ref,
                     po_ref, pl_ref, ro_ref, rl_ref, *, S):
    p = pl.program_id(1)
    W = DIL_W
    n_blocks = S // W
    qf_ref[...] = q_ref[...].astype(jnp.float32)
    kf_ref[...] = k_ref[...].astype(jnp.float32)
    vf_ref[...] = v_ref[...].astype(jnp.float32)
    @pl.when((pl.program_id(0) == 0) & (p == 0))
    def _():
        for i in range(len(DIL_PATTERNS)):
            kd_ref[i, 0:W, :] = jnp.zeros((W, LANES), jnp.bfloat16)
            vd_ref[i, 0:W, 0:LANES] = jnp.zeros((W, LANES), jnp.bfloat16)
            vd_ref[i, :, LANES:] = jnp.ones((S + W, LANES), jnp.bfloat16)

    lane = lax.broadcasted_iota(jnp.int32, (W, LANES), 1)
    first = lane < HEAD_DIM
    r_i = lax.broadcasted_iota(jnp.int32, (W, 2 * W), 0)
    c_i = lax.broadcasted_iota(jnp.int32, (W, 2 * W), 1)
    dist = W + r_i - c_i
    valid = (dist >= 0) & (dist <= W)
    dist_f = dist.astype(jnp.float32)

    for pi, (_, d) in enumerate(DIL_PATTERNS):
        L = S // d
        nbk = L // W
        qd, kd, vd, bias = qd_ref.at[pi], kd_ref.at[pi], vd_ref.at[pi], a_ref.at[pi]
        po, plse = po_ref.at[pi], pl_ref.at[pi]
        if d == 1:
            qd[...] = q_ref[...]
            kd[W:W + S, :] = k_ref[...]
            vd[W:W + S, 0:LANES] = v_ref[...]
        else:
            for r in range(d):
                qd[r * L:(r + 1) * L, :] = qf_ref[pl.ds(r, L, stride=d), :].astype(jnp.bfloat16)
                kd[W + r * L:W + (r + 1) * L, :] = kf_ref[pl.ds(r, L, stride=d), :].astype(jnp.bfloat16)
                vd[W + r * L:W + (r + 1) * L, 0:LANES] = vf_ref[pl.ds(r, L, stride=d), :].astype(jnp.bfloat16)
        for h in range(2):
            slope = slopes_ref[2 * p + h] * float(d)
            full = jnp.where(valid, slope * dist_f, jnp.inf)
            bias[0, h * W:(h + 1) * W, :] = full
            bias[1, h * W:(h + 1) * W, :] = jnp.where(c_i < W, jnp.inf, full)

        dst_o = ro_ref.at[pi] if d == 1 else po
        dst_l = rl_ref.at[pi] if d == 1 else plse

        def one_block(u, no_prev, dst_o=dst_o, dst_l=dst_l, qd=qd, kd=kd, vd=vd, bias=bias):
            row0 = u * W
            qb = qd[pl.ds(row0, W), :].astype(jnp.float32)
            kk = kd[pl.ds(row0, 2 * W), :]
            vv = vd[pl.ds(row0, 2 * W), :]
            q2 = jnp.concatenate([jnp.where(first, qb, 0.0), jnp.where(first, 0.0, qb)],
                                 axis=0).astype(jnp.bfloat16)
            s = lax.dot_general(q2, kk, (((1,), (1,)), ((), ())),
                                preferred_element_type=jnp.float32)
            s = s - bias[no_prev]
            m = jnp.max(s, axis=1, keepdims=True)
            pr = jnp.exp2(s - m)
            pv = jnp.dot(pr.astype(jnp.bfloat16), vv, preferred_element_type=jnp.float32)
            l = pv[:, LANES:]
            o = pv[:, :LANES] / l
            lse = m + jnp.log2(l)
            dst_o[pl.ds(row0, W), :] = jnp.where(first, o[:W], o[W:])
            dst_l[pl.ds(row0, W), :] = jnp.where(first, lse[:W], lse[W:])

        for u in range(n_blocks):
            one_block(u, int(u % nbk == 0))

        if d > 1:
            for r in range(d):
                ro_ref[pi, pl.ds(r, L, stride=d), :] = po[r * L:(r + 1) * L, :]
                rl_ref[pi, pl.ds(r, L, stride=d), :] = plse[r * L:(r + 1) * L, :]

    def mix(c, carry):
        row0 = pl.multiple_of(c * W, W)
        ls = [rl_ref[i, pl.ds(row0, W), :] for i in range(3)]
        mx = jnp.maximum(jnp.maximum(ls[0], ls[1]), ls[2])
        es = [jnp.exp2(x - mx) for x in ls]
        den = es[0] + es[1] + es[2]
        num = (es[0] * ro_ref[0, pl.ds(row0, W), :] + es[1] * ro_ref[1, pl.ds(row0, W), :]
               + es[2] * ro_ref[2, pl.ds(row0, W), :])
        o_ref[pl.ds(row0, W), :] = (num / den).astype(o_ref.dtype)
        return carry

    lax.fori_loop(0, n_blocks, mix, 0)


def _dil_attn(proj, slopes, B, S):
    T = proj.shape[0]
    n_pairs = N_DIL_HEADS // 2
    base = 3 * N_DIFF_HEADS * HEAD_DIM // LANES
    W = DIL_W
    return pl.pallas_call(
        functools.partial(_dil_attn_kernel, S=S),
        grid=(B, n_pairs),
        in_specs=[
            pl.BlockSpec(memory_space=pltpu.SMEM),
            pl.BlockSpec((S, LANES), lambda b, p: (b, base + p)),
            pl.BlockSpec((S, LANES), lambda b, p: (b, base + n_pairs + p)),
            pl.BlockSpec((S, LANES), lambda b, p: (b, base + 2 * n_pairs + p)),
        ],
        out_specs=pl.BlockSpec((S, LANES), lambda b, p: (b, p)),
        out_shape=jax.ShapeDtypeStruct((T, N_DIL_HEADS * HEAD_DIM), jnp.bfloat16),
        scratch_shapes=[
            pltpu.VMEM((S, LANES), jnp.float32),
            pltpu.VMEM((S, LANES), jnp.float32),
            pltpu.VMEM((S, LANES), jnp.float32),
            pltpu.VMEM((3, S, LANES), jnp.bfloat16),
            pltpu.VMEM((3, S + W, LANES), jnp.bfloat16),
            pltpu.VMEM((3, S + W, 2 * LANES), jnp.bfloat16),
            pltpu.VMEM((3, 2, 2 * W, 2 * W), jnp.float32),
            pltpu.VMEM((3, S, LANES), jnp.float32),
            pltpu.VMEM((3, S, LANES), jnp.float32),
            pltpu.VMEM((3, S, LANES), jnp.float32),
            pltpu.VMEM((3, S, LANES), jnp.float32),
        ],
        compiler_params=_cparams(("arbitrary", "arbitrary")),
        name="dil_attn",
    )(slopes, proj, proj, proj)


def _post_attn_kernel(x_ref, md_ref, ml_ref, wo_ref, nw_ref, rw_ref, rb_ref,
                      x1_ref, h2_ref, route_ref, gate_ref, cnt_ref, run_ref, tri_ref, *, tm):
    i = pl.program_id(0)

    @pl.when(i == 0)
    def _():
        run_ref[...] = jnp.zeros(run_ref.shape, jnp.float32)
        r_i = lax.broadcasted_iota(jnp.int32, (tm, tm), 0)
        c_i = lax.broadcasted_iota(jnp.int32, (tm, tm), 1)
        tri_ref[...] = jnp.where(c_i < r_i, 1.0, 0.0).astype(jnp.bfloat16)

    half = md_ref.shape[1]
    y = jnp.dot(md_ref[...], wo_ref[0:half, :], preferred_element_type=jnp.float32)
    y = y + jnp.dot(ml_ref[...], wo_ref[half:, :], preferred_element_type=jnp.float32)
    x1 = x_ref[...] + y
    x1_ref[...] = x1
    ms = jnp.mean(x1 * x1, axis=-1, keepdims=True)
    h2 = (x1 * lax.rsqrt(ms + NORM_EPS)) * nw_ref[...]
    for c in range(D_MODEL // LANES):
        h2_ref[pl.ds(c, tm, stride=SUBLANES), :] = h2[:, c * LANES:(c + 1) * LANES]
    hh = h2.astype(jnp.bfloat16)
    hl = (h2 - hh.astype(jnp.float32)).astype(jnp.bfloat16)
    lh = jnp.dot(hh, rw_ref[...], preferred_element_type=jnp.float32)
    ll = jnp.dot(hl, rw_ref[...], preferred_element_type=jnp.float32)
    logits = (lh[:, :LANES] + lh[:, LANES:]) + (ll[:, :LANES] + ll[:, LANES:]) + rb_ref[...]
    lane = lax.broadcasted_iota(jnp.int32, (tm, LANES), 1)
    work = jnp.where(lane < N_EXPERTS, logits, NEG_INF)
    vals, idxs, sels = [], [], []
    for _ in range(TOP_K):
        mk = jnp.max(work, axis=1, keepdims=True)
        ik = jnp.min(jnp.where(work == mk, lane, LANES), axis=1, keepdims=True)
        sel = lane == ik
        work = jnp.where(sel, NEG_INF, work)
        vals.append(mk)
        idxs.append(ik)
        sels.append(sel)
    es = [jnp.exp(v - vals[0]) for v in vals]
    den = es[0] + es[1] + es[2] + es[3]
    member = jnp.zeros((tm, LANES), jnp.float32)
    for sel in sels:
        member = jnp.where(sel, 1.0, member)
    before = jnp.dot(tri_ref[...], member.astype(jnp.bfloat16), preferred_element_type=jnp.float32)
    before = before + run_ref[...]
    route = jnp.zeros((tm, LANES), jnp.int32)
    gates = jnp.zeros((tm, LANES), jnp.float32)
    for k in range(TOP_K):
        rank = jnp.sum(jnp.where(sels[k], before, 0.0), axis=1, keepdims=True)
        route = jnp.where(lane == k, idxs[k], route)
        route = jnp.where(lane == TOP_K + k, rank.astype(jnp.int32), route)
        gates = jnp.where(lane == k, es[k] / den, gates)
    route_ref[...] = route
    gate_ref[...] = gates
    run_new = run_ref[...] + jnp.sum(member, axis=0, keepdims=True)
    run_ref[...] = run_new
    cnt_ref[...] = jnp.broadcast_to(run_new, cnt_ref.shape).astype(jnp.int32)


def _post_attn(x2d, mix_d, mix_l, wo_bf16, nw, rw_cat, rb, tm=512):
    T, D = x2d.shape
    half = mix_d.shape[1]
    return pl.pallas_call(
        functools.partial(_post_attn_kernel, tm=tm),
        grid=(T // tm,),
        in_specs=[
            pl.BlockSpec((tm, D), lambda i: (i, 0)),
            pl.BlockSpec((tm, half), lambda i: (i, 0)),
            pl.BlockSpec((tm, half), lambda i: (i, 0)),
            pl.BlockSpec((D, D), lambda i: (0, 0)),
            pl.BlockSpec((1, D), lambda i: (0, 0)),
            pl.BlockSpec((D, 2 * LANES), lambda i: (0, 0)),
            pl.BlockSpec((1, LANES), lambda i: (0, 0)),
        ],
        out_specs=[
            pl.BlockSpec((tm, D), lambda i: (i, 0)),
            pl.BlockSpec((tm * SUBLANES, LANES), lambda i: (i, 0)),
            pl.BlockSpec((tm, LANES), lambda i: (i, 0)),
            pl.BlockSpec((tm, LANES), lambda i: (i, 0)),
            pl.BlockSpec((8, LANES), lambda i: (0, 0)),
        ],
        out_shape=[
            jax.ShapeDtypeStruct((T, D), jnp.float32),
            jax.ShapeDtypeStruct((T * SUBLANES, LANES), jnp.float32),
            jax.ShapeDtypeStruct((T, LANES), jnp.int32),
            jax.ShapeDtypeStruct((T, LANES), jnp.float32),
            jax.ShapeDtypeStruct((8, LANES), jnp.int32),
        ],
        scratch_shapes=[pltpu.VMEM((1, LANES), jnp.float32), pltpu.VMEM((tm, tm), jnp.bfloat16)],
        compiler_params=_cparams(("arbitrary",)),
        name="post_attn",
    )(x2d, mix_d, mix_l, wo_bf16, nw, rw_cat, rb)


def _token_rows(ref, t, n_tokens=1):
    start = pl.multiple_of(t * SUBLANES, SUBLANES)
    return ref.at[pl.ds(start, n_tokens * SUBLANES), :]


def _dispatch_kernel(dest_ref, cnt_ref, pcnt_ref, pst_ref, h_ref, xs_ref, z_ref, sem, zsem, *, td):
    i = pl.program_id(0)
    n = pl.num_programs(0)

    def issue(t, carry):
        for k in range(TOP_K):
            pltpu.make_async_copy(_token_rows(h_ref, t),
                                  _token_rows(xs_ref, dest_ref[(i * td + t) * TOP_K + k]),
                                  sem).start(priority=k % 2)
        return carry

    lax.fori_loop(0, td, issue, 0)

    for k in range(TOP_K):
        pltpu.make_async_copy(h_ref, _token_rows(xs_ref, 0, td), sem).wait()

    @pl.when(i == n - 1)
    def _():
        z_ref[...] = jnp.zeros(z_ref.shape, z_ref.dtype)

        def pad_copy(row):
            return pltpu.make_async_copy(_token_rows(z_ref, 0), _token_rows(xs_ref, row), zsem)

        def per_expert(e, carry):
            base = pst_ref[e]

            def start(r, c):
                pad_copy(base + r).start()
                return c

            def wait(r, c):
                pad_copy(base + r).wait()
                return c

            lax.fori_loop(cnt_ref[e], pcnt_ref[e], start, 0)
            lax.fori_loop(cnt_ref[e], pcnt_ref[e], wait, 0)
            return carry

        lax.fori_loop(0, N_EXPERTS, per_expert, 0)

        def tail_copy(c):
            return pltpu.make_async_copy(z_ref, _token_rows(xs_ref, c * td, td), zsem)

        end = pst_ref[N_EXPERTS - 1] + pcnt_ref[N_EXPERTS - 1]
        first_chunk = end // td
        n_chunks = xs_ref.shape[0] // (td * SUBLANES)

        def tail_start(c, carry):
            tail_copy(c).start()
            return carry

        def tail_wait(c, carry):
            tail_copy(c).wait()
            return carry

        lax.fori_loop(first_chunk, n_chunks, tail_start, 0)
        lax.fori_loop(first_chunk, n_chunks, tail_wait, 0)


def _dispatch(dest_flat, counts, pcounts, pstarts, h2t, R, td=512):
    T = h2t.shape[0] // SUBLANES
    return pl.pallas_call(
        functools.partial(_dispatch_kernel, td=td),
        grid_spec=pltpu.PrefetchScalarGridSpec(
            num_scalar_prefetch=4,
            grid=(T // td,),
            in_specs=[pl.BlockSpec((td * SUBLANES, LANES), lambda i, *_: (i, 0))],
            out_specs=pl.BlockSpec(memory_space=pl.ANY),
            scratch_shapes=[
                pltpu.VMEM((td * SUBLANES, LANES), h2t.dtype),
                pltpu.SemaphoreType.DMA,
                pltpu.SemaphoreType.DMA,
            ],
        ),
        out_shape=jax.ShapeDtypeStruct((R * SUBLANES, LANES), h2t.dtype),
        compiler_params=_cparams(("arbitrary",)),
        name="dispatch",
    )(dest_flat, counts, pcounts, pstarts, h2t)


def _experts_kernel(be_ref, used_ref, xs_ref, wgu_ref, bgu_ref, wd_ref, bd_ref, o_ref,
                    wgu_bf, wd_bf, *, n_chunk):
    i = pl.program_id(0)
    prev = be_ref[jnp.maximum(i - 1, 0)]
    new_expert = jnp.logical_or(i == 0, be_ref[i] != prev)

    @pl.when(new_expert)
    def _():
        wgu_bf[...] = wgu_ref[0].astype(jnp.bfloat16)
        wd_bf[...] = wd_ref[0].astype(jnp.bfloat16)

    blk = xs_ref.shape[0] // SUBLANES
    n_lane_chunks = D_MODEL // LANES

    @pl.when(i < used_ref[0])
    def _():
        xb = jnp.concatenate(
            [xs_ref[pl.ds(c, blk, stride=SUBLANES), :].astype(jnp.bfloat16)
             for c in range(n_lane_chunks)], axis=1)
        acc = jnp.zeros((blk, D_MODEL), jnp.float32)
        for c in range(D_FF // n_chunk):
            gs = slice(c * n_chunk, (c + 1) * n_chunk)
            us = slice(D_FF + c * n_chunk, D_FF + (c + 1) * n_chunk)
            g = jnp.dot(xb, wgu_bf[:, gs], preferred_element_type=jnp.float32) + bgu_ref[0, :, gs]
            u = jnp.dot(xb, wgu_bf[:, us], preferred_element_type=jnp.float32) + bgu_ref[0, :, us]
            g = jnp.minimum(g, SWIGLU_LIMIT)
            u = jnp.clip(u, -SWIGLU_LIMIT, SWIGLU_LIMIT)
            act = (u + 1.0) * (g * jax.nn.sigmoid(SWIGLU_ALPHA * g))
            acc = acc + jnp.dot(act.astype(jnp.bfloat16), wd_bf[gs, :],
                                preferred_element_type=jnp.float32)
        res = acc + bd_ref[0]
        for c in range(n_lane_chunks):
            o_ref[pl.ds(c, blk, stride=SUBLANES), :] = res[:, c * LANES:(c + 1) * LANES]

    @pl.when(i >= used_ref[0])
    def _():
        o_ref[...] = jnp.zeros(o_ref.shape, o_ref.dtype)


def _experts(block_e, used, xs, w_gate_up, b_gate_up, w_down, b_down, blk=MOE_BLK):
    R = xs.shape[0] // SUBLANES
    D = D_MODEL
    n_blocks = R // blk
    F2 = w_gate_up.shape[2]

    def row_map(i, be, used):
        return (i, 0)

    return pl.pallas_call(
        functools.partial(_experts_kernel, n_chunk=512),
        grid_spec=pltpu.PrefetchScalarGridSpec(
            num_scalar_prefetch=2,
            grid=(n_blocks,),
            in_specs=[
                pl.BlockSpec((blk * SUBLANES, LANES), row_map),
                pl.BlockSpec((1, D, F2), lambda i, be, used: (be[i], 0, 0)),
                pl.BlockSpec((1, 1, F2), lambda i, be, used: (be[i], 0, 0)),
                pl.BlockSpec((1, D_FF, D), lambda i, be, used: (be[i], 0, 0)),
                pl.BlockSpec((1, 1, D), lambda i, be, used: (be[i], 0, 0)),
            ],
            out_specs=pl.BlockSpec((blk * SUBLANES, LANES), row_map),
            scratch_shapes=[
                pltpu.VMEM((D, F2), jnp.bfloat16),
                pltpu.VMEM((D_FF, D), jnp.bfloat16),
            ],
        ),
        out_shape=jax.ShapeDtypeStruct((R * SUBLANES, LANES), jnp.float32),
        compiler_params=_cparams(("arbitrary",)),
        name="experts",
    )(block_e, used, xs, w_gate_up, b_gate_up.reshape(N_EXPERTS, 1, F2),
      w_down, b_down.reshape(N_EXPERTS, 1, D))


def _combine_kernel(dest_ref, x1_ref, g_ref, nw_ref, rows_ref, o_ref, buf_ref, sems, *, tc):
    i = pl.program_id(0)
    n = pl.num_programs(0)
    slot = i & 1

    def issue_tile(tile, s):
        def issue(t, carry):
            for k in range(TOP_K):
                d = dest_ref[(tile * tc + t) * TOP_K + k]
                pltpu.make_async_copy(_token_rows(rows_ref, d), _token_rows(buf_ref.at[s, k], t),
                                      sems.at[s]).start(priority=k % 2)
            return carry

        lax.fori_loop(0, tc, issue, 0)

    @pl.when(i == 0)
    def _():
        issue_tile(0, 0)

    for s in range(2):
        @pl.when((i + 1 < n) & (slot == 1 - s))
        def _(s=s):
            issue_tile(i + 1, s)

    for k in range(TOP_K):
        pltpu.make_async_copy(_token_rows(rows_ref, 0, tc), buf_ref.at[slot, k],
                              sems.at[slot]).wait()

    g = g_ref[...]
    x1 = x1_ref[...]
    chunks = []
    for c in range(D_MODEL // LANES):
        yc = x1[:, c * LANES:(c + 1) * LANES]
        for k in range(TOP_K):
            yc = yc + g[:, k:k + 1] * buf_ref[slot, k, pl.ds(c, tc, stride=SUBLANES), :]
        chunks.append(yc)
    y = jnp.concatenate(chunks, axis=1)
    ms = jnp.mean(y * y, axis=-1, keepdims=True)
    o_ref[...] = (y * lax.rsqrt(ms + NORM_EPS)) * nw_ref[...]


def _combine(dest_flat, x1, gates, nw, rows, tc=512):
    T, D = x1.shape
    return pl.pallas_call(
        functools.partial(_combine_kernel, tc=tc),
        grid_spec=pltpu.PrefetchScalarGridSpec(
            num_scalar_prefetch=1,
            grid=(T // tc,),
            in_specs=[
                pl.BlockSpec((tc, D), lambda i, *_: (i, 0)),
                pl.BlockSpec((tc, LANES), lambda i, *_: (i, 0)),
                pl.BlockSpec((1, D), lambda i, *_: (0, 0)),
                pl.BlockSpec(memory_space=pl.ANY),
            ],
            out_specs=pl.BlockSpec((tc, D), lambda i, *_: (i, 0)),
            scratch_shapes=[
                pltpu.VMEM((2, TOP_K, tc * SUBLANES, LANES), jnp.float32),
                pltpu.SemaphoreType.DMA((2,)),
            ],
        ),
        out_shape=jax.ShapeDtypeStruct((T, D), jnp.float32),
        compiler_params=_cparams(("arbitrary",)),
        name="combine",
    )(dest_flat, x1, gates, nw, rows)


def kernel(x, attn_norm_w, w_in, diff_lambda_q1, diff_lambda_k1, diff_lambda_q2, diff_lambda_k2,
           diff_subln_w, w_out, ffn_norm_w, router_w, router_b, w_gate_up, b_gate_up, w_down,
           b_down, final_norm_w):
    B, S, D = x.shape
    T = B * S
    f32 = jnp.float32
    n = jnp.arange(1, N_HEADS_TOTAL + 1, dtype=f32)
    slopes = jnp.exp2(-8.0 * n / N_HEADS_TOTAL)
    diff_slopes = slopes[0::2] * LOG2E
    dil_slopes = slopes[1::2] * LOG2E
    depth = attn_norm_w.shape[0]
    assert depth == 1, "the combine kernel fuses the final norm, so exactly one layer is supported"
    x2d = x.reshape(T, D)
    diff_w = N_DIFF_HEADS * HEAD_DIM
    colscale = jnp.concatenate([
        jnp.full((diff_w,), DIFF_QK_DIM ** -0.5 * LOG2E, f32), jnp.ones((2 * diff_w,), f32),
        jnp.full((N_DIL_HEADS * HEAD_DIM,), HEAD_DIM ** -0.5 * LOG2E, f32),
        jnp.ones((2 * N_DIL_HEADS * HEAD_DIM,), f32)]).reshape(1, IN_COLS)
    n_blocks = -(-(T * TOP_K + N_EXPERTS * (MOE_BLK - 1)) // MOE_BLK)
    R = n_blocks * MOE_BLK

    for l in range(depth):
        proj = _in_proj(x2d, attn_norm_w[l].reshape(1, D), w_in[l].astype(jnp.bfloat16), colscale)
        lam_init = 0.8 - 0.6 * math.exp(-0.3 * l)
        lam = (jnp.exp(jnp.sum(diff_lambda_q1[l] * diff_lambda_k1[l]).astype(f32))
               - jnp.exp(jnp.sum(diff_lambda_q2[l] * diff_lambda_k2[l]).astype(f32))
               + lam_init).reshape(1)
        subln2 = jnp.tile(diff_subln_w[l], 2).reshape(1, LANES)
        mix_d = _diff_attn(proj, diff_slopes, lam, subln2, B, S, 1.0 - lam_init)
        mix_l = _dil_attn(proj, dil_slopes, B, S)

        rw = jnp.zeros((D, LANES), f32).at[:, :N_EXPERTS].set(router_w[l])
        rw_hi = rw.astype(jnp.bfloat16)
        rw_lo = (rw - rw_hi.astype(f32)).astype(jnp.bfloat16)
        rb = jnp.zeros((1, LANES), f32).at[0, :N_EXPERTS].set(router_b[l])
        x1, h2, route, gates, cnt = _post_attn(
            x2d, mix_d, mix_l, w_out[l].astype(jnp.bfloat16), ffn_norm_w[l].reshape(1, D),
            jnp.concatenate([rw_hi, rw_lo], axis=1), rb)

        counts = cnt[0, :N_EXPERTS]
        pcounts = (counts + MOE_BLK - 1) // MOE_BLK * MOE_BLK
        pends = jnp.cumsum(pcounts)
        pstarts = pends - pcounts
        top_i = route[:, :TOP_K]
        rank = route[:, TOP_K:2 * TOP_K]
        onehot = top_i[..., None] == jnp.arange(N_EXPERTS, dtype=jnp.int32)
        dest = jnp.sum(jnp.where(onehot, pstarts.astype(jnp.int32), 0), axis=-1) + rank
        dest_flat = dest.reshape(T * TOP_K).astype(jnp.int32)
        used = (pends[-1] // MOE_BLK).astype(jnp.int32)
        blk_start = jnp.arange(n_blocks, dtype=jnp.int32) * MOE_BLK
        be = jnp.minimum(jnp.sum(pends[None, :] <= blk_start[:, None], axis=1), N_EXPERTS - 1)
        be_last = jnp.max(jnp.where(jnp.arange(n_blocks) < used, be, 0))
        block_e = jnp.where(jnp.arange(n_blocks) < used, be, be_last).astype(jnp.int32)

        xs = _dispatch(dest_flat, counts.astype(jnp.int32), pcounts.astype(jnp.int32),
                       pstarts.astype(jnp.int32), h2, R)
        rows = _experts(block_e, used.reshape(1), xs, w_gate_up[l], b_gate_up[l], w_down[l], b_down[l])
        x2d = _combine(dest_flat, x1, gates, final_norm_w.reshape(1, D), rows)
    return x2d.reshape(B, S, D)
```
